```python
import jax, jax.numpy as jnp
from jax import lax
import numpy as np

D_MODEL = 1024
BATCH = 2
SEQ = 8192
DEPTH = 1

N_MEM = 256
EPS = 1e-6
MLSTM_HEADS = 4
MLSTM_HEAD_DIM = D_MODEL // MLSTM_HEADS
MLSTM_DIM = MLSTM_HEADS * MLSTM_HEAD_DIM
MLSTM_CHUNK = 128
CONV_WIDTH = 4
SGU_GROUPS = 8
SGU_GROUP_DIM = D_MODEL // SGU_GROUPS
SGU_DIM = SGU_GROUPS * SGU_GROUP_DIM
SGU_CHUNK = 128
N_BRANCH = 2
IN_WIDTHS = (2 * MLSTM_DIM, MLSTM_DIM, MLSTM_DIM, MLSTM_HEADS, MLSTM_HEADS, SGU_DIM, SGU_DIM, N_BRANCH * D_MODEL)
IN_DIM = sum(IN_WIDTHS)
SPLIT_POINTS = tuple(int(s) for s in np.cumsum(IN_WIDTHS)[:-1])
XATTN_HEADS = 4
XATTN_HEAD_DIM = D_MODEL // XATTN_HEADS
XATTN_DIM = XATTN_HEADS * XATTN_HEAD_DIM
MOE_GROUPS = 4
EXPERTS_PER_GROUP = 8
N_EXPERTS = MOE_GROUPS * EXPERTS_PER_GROUP
MOE_TOPK = 2
EXPERT_FF = D_MODEL // 2
MOE_BLOCK = 128

kernel_name = 'hybrid_mlstm_sgu_xattn_hmoe'


def rmsnorm(x, g):
    x32 = x.astype(jnp.float32)
    y = x32 * lax.rsqrt(jnp.mean(x32 * x32, axis=-1, keepdims=True) + EPS)
    return (y * g.astype(jnp.float32)).astype(x.dtype)


def causal_depthwise_conv(x, w, b):
    c = x.shape[-1]
    y = lax.conv_general_dilated(x, w[:, None, :].astype(x.dtype), window_strides=(1,),
                                 padding=[(CONV_WIDTH - 1, 0)],
                                 dimension_numbers=('NWC', 'WIO', 'NWC'),
                                 feature_group_count=c)
    return y + b.astype(x.dtype)


def mlstm_chunkwise(q, k, v, i_pre, f_pre):
    B, S, H, dh = q.shape
    L = MLSTM_CHUNK
    nc = S // L
    f32 = jnp.float32

    def to_chunks(a):
        return a.astype(f32).reshape(B, nc, L, H, dh).transpose(1, 0, 3, 2, 4)

    qc = to_chunks(q)
    kc = to_chunks(k) * (dh ** -0.5)
    vc = to_chunks(v)
    logf = jax.nn.log_sigmoid(f_pre.astype(f32)).reshape(B, nc, L, H).transpose(1, 0, 3, 2)
    ig = i_pre.astype(f32).reshape(B, nc, L, H).transpose(1, 0, 3, 2)
    causal = jnp.tril(jnp.ones((L, L), dtype=bool))

    def step(carry, inp):
        C, n, m = carry
        q_, k_, v_, lf, ic = inp
        b = jnp.cumsum(lf, axis=-1)
        dmat = jnp.where(causal, b[..., :, None] - b[..., None, :] + ic[..., None, :], -jnp.inf)
        inter = b + m[..., None]
        m_t = jnp.maximum(inter, jnp.max(dmat, axis=-1))
        dexp = jnp.exp(dmat - m_t[..., None])
        inter_w = jnp.exp(inter - m_t)
        s = jnp.einsum('bhtd,bhsd->bhts', q_, k_) * dexp
        num = jnp.einsum('bhts,bhsd->bhtd', s, v_) + inter_w[..., None] * jnp.einsum('bhtd,bhde->bhte', q_, C)
        den = jnp.sum(s, axis=-1) + inter_w * jnp.einsum('bhtd,bhd->bht', q_, n)
        h = num / jnp.maximum(jnp.abs(den), jnp.exp(-m_t))[..., None]
        b_last = b[..., -1]
        w_log = b_last[..., None] - b + ic
        m_new = jnp.maximum(b_last + m, jnp.max(w_log, axis=-1))
        decay = jnp.exp(b_last + m - m_new)
        ws = jnp.exp(w_log - m_new[..., None])
        C_new = decay[..., None, None] * C + jnp.einsum('bhs,bhsd,bhse->bhde', ws, k_, v_)
        n_new = decay[..., None] * n + jnp.einsum('bhs,bhsd->bhd', ws, k_)
        return (C_new, n_new, m_new), h

    init = (jnp.zeros((B, H, dh, dh), f32), jnp.zeros((B, H, dh), f32), jnp.zeros((B, H), f32))
    _, hs = lax.scan(step, init, (qc, kc, vc, logf, ig))
    return hs.transpose(1, 0, 3, 2, 4).reshape(B, S, H, dh).astype(q.dtype)


def spatial_gating(u, v, sgu_norm_g, w_s, b_s):
    B, S, _ = u.shape
    nc = S // SGU_CHUNK
    u = jax.nn.gelu(u)
    v = rmsnorm(jax.nn.gelu(v), sgu_norm_g)
    vb = v.reshape(B, nc, SGU_CHUNK, SGU_GROUPS, SGU_GROUP_DIM)
    w_causal = jnp.tril(w_s).astype(v.dtype)
    mixed = jnp.einsum('gts,bcsgd->bctgd', w_causal, vb) + b_s.T.astype(v.dtype)[:, :, None]
    return u * mixed.reshape(B, S, SGU_DIM)


def hybrid_mixer(xn, w_in, b_gate, b_if, conv_w, conv_b, mh_norm_g, sgu_norm_g, w_s, b_s, w_out):
    B, S, _ = xn.shape
    proj = xn @ w_in
    qk, v, o, i_pre, f_pre, u, sv, gates = jnp.split(proj, SPLIT_POINTS, axis=-1)
    qk = jax.nn.silu(causal_depthwise_conv(qk, conv_w, conv_b))
    q, k = jnp.split(qk, 2, axis=-1)
    b_i, b_f = jnp.split(b_if, 2)
    hd = (B, S, MLSTM_HEADS, MLSTM_HEAD_DIM)
    h = mlstm_chunkwise(q.reshape(hd), k.reshape(hd), v.reshape(hd), i_pre + b_i, f_pre + b_f)
    y_a = rmsnorm(h, mh_norm_g.reshape(MLSTM_HEADS, MLSTM_HEAD_DIM)).reshape(B, S, MLSTM_DIM) * jax.nn.sigmoid(o)
    y_b = spatial_gating(u, sv, sgu_norm_g, w_s, b_s)
    g_a, g_b = jnp.split(jax.nn.sigmoid(gates + b_gate), 2, axis=-1)
    return (g_a * y_a + g_b * y_b) @ w_out


def memory_cross_attention(xn, mn, w_xq, w_xkv, w_xo):
    B, S, _ = xn.shape
    M = mn.shape[1]
    q = (xn @ w_xq).reshape(B, S, XATTN_HEADS, XATTN_HEAD_DIM)
    k, v = jnp.split(mn @ w_xkv, 2, axis=-1)
    k = k.reshape(B, M, XATTN_HEADS, XATTN_HEAD_DIM)
    v = v.reshape(B, M, XATTN_HEADS, XATTN_HEAD_DIM)
    scores = jnp.einsum('bshd,bmhd->bhsm', q, k).astype(jnp.float32) * (XATTN_HEAD_DIM ** -0.5)
    p = jax.nn.softmax(scores, axis=-1).astype(v.dtype)
    out = jnp.einsum('bhsm,bmhd->bshd', p, v).reshape(B, S, XATTN_DIM)
    return out @ w_xo


def hierarchical_moe(xn, w_rg, b_rg, w_re, b_re, w1, w3, w2):
    B, S, D = xn.shape
    N = B * S
    xt = xn.reshape(N, D)
    g_logits = (xt @ w_rg + b_rg).astype(jnp.float32)
    g_idx = jnp.argmax(g_logits, axis=-1)
    p_g = jnp.take_along_axis(jax.nn.softmax(g_logits, axis=-1), g_idx[:, None], axis=1)
    e_logits = (xt @ w_re + b_re).astype(jnp.float32).reshape(N, MOE_GROUPS, EXPERTS_PER_GROUP)
    e_in_group = jnp.take_along_axis(e_logits, g_idx[:, None, None], axis=1)[:, 0]
    top_v, top_i = lax.top_k(e_in_group, MOE_TOPK)
    gate = p_g * jax.nn.softmax(top_v, axis=-1)
    expert = g_idx[:, None].astype(jnp.int32) * EXPERTS_PER_GROUP + top_i.astype(jnp.int32)

    A = N * MOE_TOPK
    e_flat = expert.reshape(A)
    tok_flat = jnp.repeat(jnp.arange(N, dtype=jnp.int32), MOE_TOPK)
    order = jnp.argsort(e_flat)
    e_sorted = e_flat[order]
    counts = jnp.bincount(e_flat, length=N_EXPERTS)
    starts = jnp.cumsum(counts) - counts
    padded = (counts + MOE_BLOCK - 1) // MOE_BLOCK * MOE_BLOCK
    pends = jnp.cumsum(padded)
    pstarts = pends - padded
    dest_sorted = (pstarts[e_sorted] + jnp.arange(A, dtype=jnp.int32) - starts[e_sorted]).astype(jnp.int32)
    n_blocks = -(-A // MOE_BLOCK) + N_EXPERTS
    P = n_blocks * MOE_BLOCK
    buf_tok = jnp.zeros((P,), jnp.int32).at[dest_sorted].set(tok_flat[order])
    block_expert = jnp.minimum(
        jnp.searchsorted(pends, jnp.arange(n_blocks, dtype=jnp.int32) * MOE_BLOCK, side='right'),
        N_EXPERTS - 1).astype(jnp.int32)

    def block_ffn(args):
        tok, e = args
        xb = xt[tok]
        hb = jax.nn.silu(xb @ w1[e]) * (xb @ w3[e])
        return hb @ w2[e]

    out_buf = lax.map(block_ffn, (buf_tok.reshape(n_blocks, MOE_BLOCK), block_expert)).reshape(P, D)
    dest = jnp.zeros((A,), jnp.int32).at[order].set(dest_sorted)
    y_assign = out_buf[dest].reshape(N, MOE_TOPK, D)
    y = jnp.einsum('nk,nkd->nd', gate.astype(xt.dtype), y_assign)
    return y.reshape(B, S, D)


def setup_inputs(seed: int = 0) -> dict:
    key = jax.random.key(seed)
    ks = jax.random.split(key, 32)
    D, L = D_MODEL, DEPTH

    def nrm(k, shape, scale):
        return jax.random.normal(k, shape, jnp.float32) * scale

    def gain(k, shape):
        return 1.0 + 0.02 * jax.random.normal(k, shape, jnp.float32)

    b_i = nrm(ks[4], (L, MLSTM_HEADS), 0.1)
    b_f = jnp.linspace(3.0, 6.0, MLSTM_HEADS, dtype=jnp.float32)[None, :] + nrm(ks[5], (L, MLSTM_HEADS), 0.1)
    return {
        'x': nrm(ks[0], (BATCH, SEQ, D), 1.0),
        'mem': nrm(ks[1], (BATCH, N_MEM, D), 1.0),
        'norm_mix_g': gain(ks[2], (L, D)),
        'w_in': nrm(ks[3], (L, D, IN_DIM), D ** -0.5),
        'b_gate': nrm(ks[6], (L, N_BRANCH * D), 0.02),
        'b_if': jnp.concatenate([b_i, b_f], axis=-1),
        'conv_w': nrm(ks[7], (L, CONV_WIDTH, 2 * MLSTM_DIM), CONV_WIDTH ** -0.5),
        'conv_b': nrm(ks[8], (L, 2 * MLSTM_DIM), 0.02),
        'mh_norm_g': gain(ks[9], (L, MLSTM_DIM)),
        'sgu_norm_g': gain(ks[10], (L, SGU_DIM)),
        'w_s': nrm(ks[11], (L, SGU_GROUPS, SGU_CHUNK, SGU_CHUNK), SGU_CHUNK ** -0.5),
        'b_s': gain(ks[12], (L, SGU_GROUPS, SGU_CHUNK)),
        'w_out': nrm(ks[13], (L, D, D), D ** -0.5),
        'norm_x_g': gain(ks[14], (L, D)),
        'norm_mem_g': gain(ks[15], (L, D)),
        'w_xq': nrm(ks[16], (L, D, XATTN_DIM), D ** -0.5),
        'w_xkv': nrm(ks[17], (L, D, 2 * XATTN_DIM), D ** -0.5),
        'w_xo': nrm(ks[18], (L, XATTN_DIM, D), XATTN_DIM ** -0.5),
        'norm_moe_g': gain(ks[19], (L, D)),
        'w_rg': nrm(ks[20], (L, D, MOE_GROUPS), D ** -0.5),
        'b_rg': nrm(ks[21], (L, MOE_GROUPS), 0.01),
        'w_re': nrm(ks[22], (L, D, N_EXPERTS), D ** -0.5),
        'b_re': nrm(ks[23], (L, N_EXPERTS), 0.01),
        'w1': nrm(ks[24], (L, N_EXPERTS, D, EXPERT_FF), D ** -0.5),
        'w3': nrm(ks[25], (L, N_EXPERTS, D, EXPERT_FF), D ** -0.5),
        'w2': nrm(ks[26], (L, N_EXPERTS, EXPERT_FF, D), EXPERT_FF ** -0.5),
        'norm_f_g': gain(ks[27], (D,)),
    }


def reference(x, mem, norm_mix_g, w_in, b_gate, b_if, conv_w, conv_b, mh_norm_g, sgu_norm_g, w_s, b_s, w_out,
              norm_x_g, norm_mem_g, w_xq, w_xkv, w_xo, norm_moe_g, w_rg, b_rg, w_re, b_re, w1, w3, w2, norm_f_g):
    for l in range(DEPTH):
        x = x + hybrid_mixer(rmsnorm(x, norm_mix_g[l]), w_in[l], b_gate[l], b_if[l], conv_w[l], conv_b[l],
                             mh_norm_g[l], sgu_norm_g[l], w_s[l], b_s[l], w_out[l])
        x = x + memory_cross_attention(rmsnorm(x, norm_x_g[l]), rmsnorm(mem, norm_mem_g[l]),
                                       w_xq[l], w_xkv[l], w_xo[l])
        x = x + hierarchical_moe(rmsnorm(x, norm_moe_g[l]), w_rg[l], b_rg[l], w_re[l], b_re[l],
                                 w1[l], w3[l], w2[l])
    return rmsnorm(x, norm_f_g)
```

```python
import functools

import jax
import jax.numpy as jnp
from jax import lax
from jax.experimental import pallas as pl
from jax.experimental.pallas import tpu as pltpu

F32 = jnp.float32
BF16 = jnp.bfloat16
I32 = jnp.int32

EPS = 1e-6
LANES = 128
CHUNK = 128
MLSTM_HEADS = 4
SGU_GROUPS = 8
XATTN_HEADS = 4
MOE_GROUPS = 4
EXPERTS_PER_GROUP = 8
N_EXPERTS = MOE_GROUPS * EXPERTS_PER_GROUP
MOE_TOPK = 2
CONV_WIDTH = 4
CONV_HALO = 8
ROW_BLOCK = 256
VMEM_LIMIT = 56 * 1024 * 1024


def _sigmoid(x):
    return 1.0 / (1.0 + jnp.exp(-x))


def _rms(x, g):
    return x * lax.rsqrt(jnp.mean(x * x, axis=-1, keepdims=True) + EPS) * g


def _log_sigmoid(x):
    return jnp.minimum(x, 0.0) - jnp.log(1.0 + jnp.exp(-jnp.abs(x)))


def _gelu_tanh(x):
    c = 0.7978845608028654
    return 0.5 * x * (1.0 + jnp.tanh(c * (x + 0.044715 * (x * x * x))))


def _split3_dot(a, tri, dims):
    a_hi = a.astype(BF16)
    r1 = a - a_hi.astype(F32)
    a_mid = r1.astype(BF16)
    a_lo = (r1 - a_mid.astype(F32)).astype(BF16)
    if dims == "at":
        f = lambda p: jnp.dot(p, tri, preferred_element_type=F32)
    else:
        f = lambda p: jnp.dot(tri, p, preferred_element_type=F32)
    return f(a_hi) + f(a_mid) + f(a_lo)


def _inproj_kernel(x_ref, g_ref, w_ref, wif_ref, wift_ref, bc_ref, br_ref,
                   proj_ref, gc_ref, gr_ref, xn_ref):
    @pl.when(pl.program_id(1) == 0)
    def _():
        xn = _rms(x_ref[...], g_ref[...]).astype(BF16)
        xn_ref[...] = xn
        gc_ref[...] = jnp.dot(xn, wif_ref[...], preferred_element_type=F32) + bc_ref[...]
        gr_ref[...] = lax.dot_general(wift_ref[...], xn, (((1,), (1,)), ((), ())),
                                      preferred_element_type=F32) + br_ref[...]

    proj_ref[...] = jnp.dot(xn_ref[...], w_ref[...], preferred_element_type=F32).astype(BF16)


def _inproj(x2d, g, w_main, w_if, w_ift, b_col, b_row, tm, tn):
    n, d = x2d.shape
    nc = w_main.shape[1]
    return pl.pallas_call(
        _inproj_kernel,
        grid=(n // tm, nc // tn),
        in_specs=[
            pl.BlockSpec((tm, d), lambda i, j: (i, 0)),
            pl.BlockSpec((1, d), lambda i, j: (0, 0)),
            pl.BlockSpec((d, tn), lambda i, j: (0, j)),
            pl.BlockSpec((d, LANES), lambda i, j: (0, 0)),
            pl.BlockSpec((8, d), lambda i, j: (0, 0)),
            pl.BlockSpec((1, LANES), lambda i, j: (0, 0)),
            pl.BlockSpec((8, 1), lambda i, j: (0, 0)),
        ],
        out_specs=[
            pl.BlockSpec((tm, tn), lambda i, j: (i, j)),
            pl.BlockSpec((tm, LANES), lambda i, j: (i, 0)),
            pl.BlockSpec((8, tm), lambda i, j: (0, i)),
        ],
        out_shape=[
            jax.ShapeDtypeStruct((n, nc), BF16),
            jax.ShapeDtypeStruct((n, LANES), F32),
            jax.ShapeDtypeStruct((8, n), F32),
        ],
        scratch_shapes=[pltpu.VMEM((tm, d), BF16)],
        compiler_params=pltpu.CompilerParams(
            dimension_semantics=("arbitrary", "arbitrary"), vmem_limit_bytes=VMEM_LIMIT),
        name="inproj",
    )(x2d, g, w_main, w_if, w_ift, b_col, b_row)


def _mlstm_kernel(qk_ref, v_ref, o_ref, gc_ref, gr_ref, cw_ref, cb_ref, mhg_ref,
                  ya_ref, xbuf, c_ref, n_ref, m_ref):
    L = CHUNK
    H = MLSTM_HEADS
    dqk = qk_ref.shape[-1]
    dh = dqk // (2 * H)

    @pl.when(pl.program_id(1) == 0)
    def _():
        xbuf[0:CONV_HALO, :] = jnp.zeros((CONV_HALO, dqk), F32)
        c_ref[...] = jnp.zeros_like(c_ref)
        n_ref[...] = jnp.zeros_like(n_ref)
        m_ref[...] = jnp.zeros_like(m_ref)

    xbuf[CONV_HALO:CONV_HALO + L, :] = qk_ref[0].astype(F32)
    acc = jnp.broadcast_to(cb_ref[...], (L, dqk))
    for j in range(CONV_WIDTH):
        off = CONV_HALO + j - (CONV_WIDTH - 1)
        acc = acc + cw_ref[j:j + 1, :] * xbuf[off:off + L, :]
    act = acc * _sigmoid(acc)
    xbuf[0:CONV_HALO, :] = xbuf[L:L + CONV_HALO, :]

    gc = gc_ref[0]
    gr = gr_ref[...]
    row = lax.broadcasted_iota(I32, (L, L), 0)
    col = lax.broadcasted_iota(I32, (L, L), 1)
    causal = row >= col
    tril = causal.astype(BF16)
    triu = (row <= col).astype(BF16)
    b_c = _split3_dot(_log_sigmoid(gc), tril, "ta")
    b_r = _split3_dot(_log_sigmoid(gr), triu, "at")

    for h in range(H):
        q = act[:, h * dh:(h + 1) * dh]
        k = act[:, (H + h) * dh:(H + h + 1) * dh] * (dh ** -0.5)
        v = v_ref[0, :, h * dh:(h + 1) * dh]
        bcol = b_c[:, H + h:H + h + 1]
        icol = gc[:, h:h + 1]
        brow = b_r[H + h:H + h + 1, :]
        irow = gr[h:h + 1, :]
        m_prev = m_ref[h:h + 1, 0:1]

        dmat = jnp.where(causal, bcol - brow + irow, -jnp.inf)
        inter = bcol + m_prev
        m_t = jnp.maximum(inter, jnp.max(dmat, axis=-1, keepdims=True))
        dexp = jnp.exp(dmat - m_t)
        inter_w = jnp.exp(inter - m_t)
        qb = q.astype(BF16)
        kb = k.astype(BF16)
        s = lax.dot_general(qb, kb, (((1,), (1,)), ((), ())), preferred_element_type=F32) * dexp
        c_old = c_ref[h]
        n_old = n_ref[h:h + 1, :]
        num = (jnp.dot(s.astype(BF16), v, preferred_element_type=F32)
               + inter_w * jnp.dot(qb, c_old.astype(BF16), preferred_element_type=F32))
        den = (jnp.sum(s, axis=-1, keepdims=True)
               + inter_w * jnp.sum(q * n_old, axis=-1, keepdims=True))
        hh = num * (1.0 / jnp.maximum(jnp.abs(den), jnp.exp(-m_t)))

        b_last = bcol[L - 1:L, :]
        wlog = b_last - bcol + icol
        m_new = jnp.maximum(b_last + m_prev, jnp.max(wlog, axis=0, keepdims=True))
        decay = jnp.exp(b_last + m_prev - m_new)
        kw = k * jnp.exp(wlog - m_new)
        c_ref[h] = decay * c_old + jnp.dot(kw.T.astype(BF16), v, preferred_element_type=F32)
        n_ref[h:h + 1, :] = decay * n_old + jnp.sum(kw, axis=0, keepdims=True)
        m_ref[h:h + 1, :] = jnp.broadcast_to(m_new, (1, LANES))

        y = _rms(hh, mhg_ref[:, h * dh:(h + 1) * dh])
        og = _sigmoid(o_ref[0, :, h * dh:(h + 1) * dh].astype(F32))
        ya_ref[0, :, h * dh:(h + 1) * dh] = (y * og).astype(BF16)


def _mlstm(proj3, gc3, gr, conv_w, conv_b, mh_g, d):
    b, s, _ = proj3.shape
    nc = s // CHUNK
    dqk = 2 * d
    dh = d // MLSTM_HEADS
    return pl.pallas_call(
        _mlstm_kernel,
        grid=(b, nc),
        in_specs=[
            pl.BlockSpec((1, CHUNK, dqk), lambda bi, ci: (bi, ci, 0)),
            pl.BlockSpec((1, CHUNK, d), lambda bi, ci: (bi, ci, 2)),
            pl.BlockSpec((1, CHUNK, d), lambda bi, ci: (bi, ci, 3)),
            pl.BlockSpec((1, CHUNK, LANES), lambda bi, ci: (bi, ci, 0)),
            pl.BlockSpec((8, CHUNK), lambda bi, ci: (0, bi * nc + ci)),
            pl.BlockSpec((CONV_WIDTH, dqk), lambda bi, ci: (0, 0)),
            pl.BlockSpec((1, dqk), lambda bi, ci: (0, 0)),
            pl.BlockSpec((1, d), lambda bi, ci: (0, 0)),
        ],
        out_specs=pl.BlockSpec((1, CHUNK, d), lambda bi, ci: (bi, ci, 0)),
        out_shape=jax.ShapeDtypeStruct((b, s, d), BF16),
        scratch_shapes=[
            pltpu.VMEM((CHUNK + CONV_HALO, dqk), F32),
            pltpu.VMEM((MLSTM_HEADS, dh, dh), F32),
            pltpu.VMEM((8, dh), F32),
            pltpu.VMEM((8, LANES), F32),
        ],
        compiler_params=pltpu.CompilerParams(
            dimension_semantics=("arbitrary", "arbitrary"), vmem_limit_bytes=VMEM_LIMIT),
        name="mlstm",
    )(proj3, proj3, proj3, gc3, gr, conv_w, conv_b, mh_g)


def _merge_kernel(u_ref, sv_ref, ga_ref, gb_ref, ya_ref, x_ref, ws_ref, bst_ref, sg_ref, bg_ref,
                  wout_ref, x1_ref, yb_ref):
    tm, d = x_ref.shape
    L = CHUNK
    G = SGU_GROUPS
    gd = d // G
    uu = _gelu_tanh(u_ref[...].astype(F32))
    vn = _rms(_gelu_tanh(sv_ref[...].astype(F32)), sg_ref[...]).astype(BF16)
    row = lax.broadcasted_iota(I32, (L, L), 0)
    col = lax.broadcasted_iota(I32, (L, L), 1)
    causal = row >= col
    for g in range(G):
        wg = jnp.where(causal, ws_ref[g], 0.0).astype(BF16)
        bias = bst_ref[:, g:g + 1]
        for ci in range(tm // L):
            mixed = jnp.dot(wg, vn[ci * L:(ci + 1) * L, g * gd:(g + 1) * gd],
                            preferred_element_type=F32)
            yb_ref[ci * L:(ci + 1) * L, g * gd:(g + 1) * gd] = (
                uu[ci * L:(ci + 1) * L, g * gd:(g + 1) * gd] * (mixed + bias))
    g_a = _sigmoid(ga_ref[...].astype(F32) + bg_ref[:, 0:d])
    g_b = _sigmoid(gb_ref[...].astype(F32) + bg_ref[:, d:2 * d])
    merged = (g_a * ya_ref[...].astype(F32) + g_b * yb_ref[...]).astype(BF16)
    x1_ref[...] = x_ref[...] + jnp.dot(merged, wout_ref[...], preferred_element_type=F32)


def _merge(proj, ya, x2d, w_s, b_st, sgu_g, b_gate, w_out, tm):
    n, d = x2d.shape
    full = lambda shape: pl.BlockSpec(shape, lambda i: (0,) * len(shape))
    return pl.pallas_call(
        _merge_kernel,
        grid=(n // tm,),
        in_specs=[
            pl.BlockSpec((tm, d), lambda i: (i, 4)),
            pl.BlockSpec((tm, d), lambda i: (i, 5)),
            pl.BlockSpec((tm, d), lambda i: (i, 6)),
            pl.BlockSpec((tm, d), lambda i: (i, 7)),
            pl.BlockSpec((tm, d), lambda i: (i, 0)),
            pl.BlockSpec((tm, d), lambda i: (i, 0)),
            full(w_s.shape), full(b_st.shape), full(sgu_g.shape), full(b_gate.shape), full(w_out.shape),
        ],
        out_specs=pl.BlockSpec((tm, d), lambda i: (i, 0)),
        out_shape=jax.ShapeDtypeStruct((n, d), F32),
        scratch_shapes=[pltpu.VMEM((tm, d), F32)],
        compiler_params=pltpu.CompilerParams(
            dimension_semantics=("arbitrary",), vmem_limit_bytes=VMEM_LIMIT),
        name="merge",
    )(proj, proj, proj, proj, ya, x2d, w_s, b_st, sgu_g, b_gate, w_out)


def _memkv_kernel(mem_ref, g_ref, w_ref, kv_ref):
    mn = _rms(mem_ref[0], g_ref[...]).astype(BF16)
    kv_ref[0] = jnp.dot(mn, w_ref[...], preferred_element_type=F32).astype(BF16)


def _memkv(mem, g, w_xkv):
    b, m, d = mem.shape
    dk = w_xkv.shape[1]
    return pl.pallas_call(
        _memkv_kernel,
        grid=(b,),
        in_specs=[
            pl.BlockSpec((1, m, d), lambda i: (i, 0, 0)),
            pl.BlockSpec((1, d), lambda i: (0, 0)),
            pl.BlockSpec((d, dk), lambda i: (0, 0)),
        ],
        out_specs=pl.BlockSpec((1, m, dk), lambda i: (i, 0, 0)),
        out_shape=jax.ShapeDtypeStruct((b, m, dk), BF16),
        compiler_params=pltpu.CompilerParams(
            dimension_semantics=("arbitrary",), vmem_limit_bytes=VMEM_LIMIT),
        name="memkv",
    )(mem, g, w_xkv)


def _xattn_kernel(x1_ref, kv_ref, gx_ref, wq_ref, wo_ref, gm_ref, wr_ref, br_ref,
                  x2_ref, xn2_ref, lg_ref, att_ref):
    tm, d = x1_ref.shape[1:]
    H = XATTN_HEADS
    dh = d // H
    x1 = x1_ref[0]
    xn = _rms(x1, gx_ref[...]).astype(BF16)
    q = (jnp.dot(xn, wq_ref[...], preferred_element_type=F32) * (dh ** -0.5)).astype(BF16)
    for h in range(H):
        kh = kv_ref[0, :, h * dh:(h + 1) * dh]
        vh = kv_ref[0, :, d + h * dh:d + (h + 1) * dh]
        s = lax.dot_general(q[:, h * dh:(h + 1) * dh], kh, (((1,), (1,)), ((), ())),
                            preferred_element_type=F32)
        e = jnp.exp(s - jnp.max(s, axis=-1, keepdims=True))
        p = e * (1.0 / jnp.sum(e, axis=-1, keepdims=True))
        att_ref[:, h * dh:(h + 1) * dh] = jnp.dot(p.astype(BF16), vh,
                                                 preferred_element_type=F32).astype(BF16)
    x2 = x1 + jnp.dot(att_ref[...], wo_ref[...], preferred_element_type=F32)
    x2_ref[0] = x2
    xn2 = _rms(x2, gm_ref[...])
    xn2_ref[0] = xn2
    lg_ref[0] = _split3_both(xn2, wr_ref[...]) + br_ref[...]


def _split3_both(a, b):
    a_hi = a.astype(BF16)
    a_lo = (a - a_hi.astype(F32)).astype(BF16)
    b_hi = b.astype(BF16)
    b_lo = (b - b_hi.astype(F32)).astype(BF16)
    f = lambda p, r: jnp.dot(p, r, preferred_element_type=F32)
    return f(a_hi, b_hi) + (f(a_hi, b_lo) + f(a_lo, b_hi))


def _xattn(x1_3d, kv, gx, w_xq, w_xo, gm, w_r, b_r, tm):
    b, s, d = x1_3d.shape
    m = kv.shape[1]
    full = lambda shape: pl.BlockSpec(shape, lambda bi, i: (0,) * len(shape))
    return pl.pallas_call(
        _xattn_kernel,
        grid=(b, s // tm),
        in_specs=[
            pl.BlockSpec((1, tm, d), lambda bi, i: (bi, i, 0)),
            pl.BlockSpec((1, m, 2 * d), lambda bi, i: (bi, 0, 0)),
            full(gx.shape), full(w_xq.shape), full(w_xo.shape), full(gm.shape), full(w_r.shape),
            full(b_r.shape),
        ],
        out_specs=[
            pl.BlockSpec((1, tm, d), lambda bi, i: (bi, i, 0)),
            pl.BlockSpec((1, tm, d), lambda bi, i: (bi, i, 0)),
            pl.BlockSpec((1, tm, LANES), lambda bi, i: (bi, i, 0)),
        ],
        out_shape=[
            jax.ShapeDtypeStruct((b, s, d), F32),
            jax.ShapeDtypeStruct((b, s, d), F32),
            jax.ShapeDtypeStruct((b, s, LANES), F32),
        ],
        scratch_shapes=[pltpu.VMEM((tm, d), BF16)],
        compiler_params=pltpu.CompilerParams(
            dimension_semantics=("arbitrary", "arbitrary"), vmem_limit_bytes=VMEM_LIMIT),
        name="xattn",
    )(x1_3d, kv, gx, w_xq, w_xo, gm, w_r, b_r)


def _router_kernel(lg_ref, ids_ref, gate_ref, cnt_ref, carry_ref):
    tm = lg_ref.shape[0]
    G, E = MOE_GROUPS, N_EXPERTS

    @pl.when(pl.program_id(0) == 0)
    def _():
        carry_ref[...] = jnp.zeros_like(carry_ref)

    lg = lg_ref[...]
    lane = lax.broadcasted_iota(I32, (tm, LANES), 1)
    lane_f = lane.astype(F32)
    ninf = -jnp.inf

    def first_lane(mask):
        return jnp.min(jnp.where(mask, lane_f, float(LANES)), axis=-1, keepdims=True).astype(I32)

    glm = jnp.where(lane < G, lg, ninf)
    gmax = jnp.max(glm, axis=-1, keepdims=True)
    g_idx = first_lane(glm == gmax)
    p_g = 1.0 / jnp.sum(jnp.exp(glm - gmax), axis=-1, keepdims=True)

    in_group = (lane >= G) & (lane < G + E) & (((lane - G) >> 3) == g_idx)
    em = jnp.where(in_group, lg, ninf)
    top1 = jnp.max(em, axis=-1, keepdims=True)
    i1 = first_lane(em == top1)
    em2 = jnp.where(lane == i1, ninf, em)
    top2 = jnp.max(em2, axis=-1, keepdims=True)
    i2 = first_lane(em2 == top2)
    e2 = jnp.exp(top2 - top1)
    inv = 1.0 / (1.0 + e2)
    gate1 = p_g * inv
    gate2 = p_g * (e2 * inv)

    hot1 = lane == i1
    hot2 = lane == i2
    onehot = (hot1 | hot2).astype(BF16)
    r = lax.broadcasted_iota(I32, (tm, tm), 0)
    c = lax.broadcasted_iota(I32, (tm, tm), 1)
    strict = (r > c).astype(BF16)
    before = jnp.dot(strict, onehot, preferred_element_type=F32) + carry_ref[...]
    rank1 = jnp.sum(jnp.where(hot1, before, 0.0), axis=-1, keepdims=True)
    rank2 = jnp.sum(jnp.where(hot2, before, 0.0), axis=-1, keepdims=True)
    carry_ref[...] = carry_ref[...] + jnp.sum(onehot.astype(F32), axis=0, keepdims=True)

    ids = jnp.where(lane == 0, i1 - G, 0)
    ids = jnp.where(lane == 1, i2 - G, ids)
    ids = jnp.where(lane == 2, rank1.astype(I32), ids)
    ids = jnp.where(lane == 3, rank2.astype(I32), ids)
    ids_ref[...] = ids
    gate_ref[...] = jnp.where(lane == 0, gate1, jnp.where(lane == 1, gate2, 0.0))
    cnt_ref[...] = carry_ref[...]


def _router(logits, tm):
    n = logits.shape[0]
    return pl.pallas_call(
        _router_kernel,
        grid=(n // tm,),
        in_specs=[pl.BlockSpec((tm, LANES), lambda i: (i, 0))],
        out_specs=[
            pl.BlockSpec((tm, LANES), lambda i: (i, 0)),
            pl.BlockSpec((tm, LANES), lambda i: (i, 0)),
            pl.BlockSpec((1, LANES), lambda i: (0, 0)),
        ],
        out_shape=[
            jax.ShapeDtypeStruct((n, LANES), I32),
            jax.ShapeDtypeStruct((n, LANES), F32),
            jax.ShapeDtypeStruct((1, LANES), F32),
        ],
        scratch_shapes=[pltpu.VMEM((1, LANES), F32)],
        compiler_params=pltpu.CompilerParams(dimension_semantics=("arbitrary",)),
        name="router",
    )(logits)


def _ffn_kernel(bexp_ref, rows_ref, nvalid_ref, slot_hbm, src_hbm, w1_ref, w3_ref, w2_ref, y_hbm,
                idx_smem, xbuf, obuf, w1b, w3b, w2b, isem, gsem, ssem):
    i = pl.program_id(0)
    tb = ROW_BLOCK
    n_tok = src_hbm.shape[0]
    nvalid = nvalid_ref[0]
    cur = i % 2
    nxt = 1 - cur

    def idx_copy(blk, buf):
        return pltpu.make_async_copy(slot_hbm.at[blk], idx_smem.at[buf], isem.at[buf])

    def gather_row(buf, r):
        slot = idx_smem[buf, r]
        tok = jnp.where(slot >= n_tok, slot - n_tok, slot)
        return pltpu.make_async_copy(src_hbm.at[pl.ds(tok, 1), :], xbuf.at[buf, pl.ds(r, 1), :],
                                     gsem.at[buf])

    def scatter_row(buf, r):
        slot = idx_smem[buf, r]
        return pltpu.make_async_copy(obuf.at[buf, pl.ds(r, 1), :], y_hbm.at[pl.ds(slot, 1), :],
                                     ssem.at[buf])

    def start_gathers(buf):
        def body(r8, carry):
            for u in range(8):
                gather_row(buf, r8 * 8 + u).start()
            return carry
        lax.fori_loop(0, tb // 8, body, 0)

    def wait_gathers(buf):
        def body(r, carry):
            gather_row(buf, r).wait()
            return carry
        lax.fori_loop(0, tb, body, 0)

    def start_scatters(buf, nrows):
        def body(r, carry):
            scatter_row(buf, r).start()
            return carry
        lax.fori_loop(0, nrows, body, 0)

    def wait_scatters(buf, nrows):
        def body(r, carry):
            scatter_row(buf, r).wait()
            return carry
        lax.fori_loop(0, nrows, body, 0)

    @pl.when(i == 0)
    def _():
        c = idx_copy(0, 0)
        c.start()
        c.wait()
        start_gathers(0)

    @pl.when(i < nvalid)
    def _():
        wait_gathers(cur)

        @pl.when(i >= 1)
        def _():
            wait_scatters(nxt, rows_ref[jnp.maximum(i - 1, 0)])

        @pl.when(i + 1 < nvalid)
        def _():
            c = idx_copy(i + 1, nxt)
            c.start()
            c.wait()
            start_gathers(nxt)

        changed = jnp.logical_or(i == 0, bexp_ref[i] != bexp_ref[jnp.maximum(i - 1, 0)])

        @pl.when(changed)
        def _():
            w1b[...] = w1_ref[0].astype(BF16)
            w3b[...] = w3_ref[0].astype(BF16)
            w2b[...] = w2_ref[0].astype(BF16)

        xb = xbuf[cur].astype(BF16)
        a = jnp.dot(xb, w1b[...], preferred_element_type=F32)
        g = jnp.dot(xb, w3b[...], preferred_element_type=F32)
        hb = (a * _sigmoid(a) * g).astype(BF16)
        obuf[cur] = jnp.dot(hb, w2b[...], preferred_element_type=F32)
        start_scatters(cur, rows_ref[i])

        @pl.when(i == nvalid - 1)
        def _():
            wait_scatters(cur, rows_ref[i])


def _ffn(block_expert, block_rows, nvalid, slots, xn2, w1, w3, w2, n_rows_out):
    nb, tb = slots.shape
    n, d = xn2.shape
    e, _, f = w1.shape
    grid_spec = pltpu.PrefetchScalarGridSpec(
        num_scalar_prefetch=3,
        grid=(nb,),
        in_specs=[
            pl.BlockSpec(memory_space=pl.ANY),
            pl.BlockSpec(memory_space=pl.ANY),
            pl.BlockSpec((1, d, f), lambda i, be, br, nv: (be[i], 0, 0)),
            pl.BlockSpec((1, d, f), lambda i, be, br, nv: (be[i], 0, 0)),
            pl.BlockSpec((1, f, d), lambda i, be, br, nv: (be[i], 0, 0)),
        ],
        out_specs=pl.BlockSpec(memory_space=pl.ANY),
        scratch_shapes=[
            pltpu.SMEM((2, tb), I32),
            pltpu.VMEM((2, tb, d), F32),
            pltpu.VMEM((2, tb, d), F32),
            pltpu.VMEM((d, f), BF16),
            pltpu.VMEM((d, f), BF16),
            pltpu.VMEM((f, d), BF16),
            pltpu.SemaphoreType.DMA((2,)),
            pltpu.SemaphoreType.DMA((2,)),
            pltpu.SemaphoreType.DMA((2,)),
        ],
    )
    return pl.pallas_call(
        _ffn_kernel,
        grid_spec=grid_spec,
        out_shape=jax.ShapeDtypeStruct((n_rows_out, d), F32),
        compiler_params=pltpu.CompilerParams(
            dimension_semantics=("arbitrary",), vmem_limit_bytes=VMEM_LIMIT),
        name="ffn",
    )(block_expert, block_rows, nvalid, slots, xn2, w1, w3, w2)


def _combine_kernel(x2_ref, y0_ref, y1_ref, gate_ref, g_ref, out_ref):
    gate = gate_ref[...]
    x = x2_ref[...] + gate[:, 0:1] * y0_ref[...] + gate[:, 1:2] * y1_ref[...]
    out_ref[...] = _rms(x, g_ref[...])


def _combine(x2, y, gates, g, tm):
    n, d = x2.shape
    nt = n // tm
    return pl.pallas_call(
        _combine_kernel,
        grid=(nt,),
        in_specs=[
            pl.BlockSpec((tm, d), lambda i: (i, 0)),
            pl.BlockSpec((tm, d), lambda i: (i, 0)),
            pl.BlockSpec((tm, d), lambda i: (nt + i, 0)),
            pl.BlockSpec((tm, LANES), lambda i: (i, 0)),
            pl.BlockSpec((1, d), lambda i: (0, 0)),
        ],
        out_specs=pl.BlockSpec((tm, d), lambda i: (i, 0)),
        out_shape=jax.ShapeDtypeStruct((n, d), F32),
        compiler_params=pltpu.CompilerParams(
            dimension_semantics=("arbitrary",), vmem_limit_bytes=VMEM_LIMIT),
        name="combine",
    )(x2, y, y, gates, g)


def _layer(x, mem, norm_mix_g, w_in, b_gate, b_if, conv_w, conv_b, mh_norm_g, sgu_norm_g, w_s, b_s,
           w_out, norm_x_g, norm_mem_g, w_xq, w_xkv, w_xo, norm_moe_g, w_rg, b_rg, w_re, b_re,
           w1, w3, w2):
    b, s, d = x.shape
    n = b * s
    H = MLSTM_HEADS
    x2d = x.reshape(n, d)
    row = lambda v: v.reshape(1, -1)

    c_if = 4 * d
    w_main = jnp.concatenate([w_in[:, :c_if], w_in[:, c_if + 2 * H:]], axis=1).astype(BF16)
    w_if = w_in[:, c_if:c_if + 2 * H]
    w_if_col = jnp.pad(w_if, ((0, 0), (0, LANES - 2 * H))).astype(BF16)
    w_if_row = w_if.T.astype(BF16)
    b_col = jnp.pad(b_if, (0, LANES - 2 * H)).reshape(1, LANES)
    b_row = b_if.reshape(2 * H, 1)

    proj, gc, gr = _inproj(x2d, row(norm_mix_g), w_main, w_if_col, w_if_row, b_col, b_row,
                           tm=min(1024, n), tn=1024)
    ya = _mlstm(proj.reshape(b, s, -1), gc.reshape(b, s, LANES), gr, conv_w, row(conv_b),
                row(mh_norm_g), d)
    x1 = _merge(proj, ya.reshape(n, d), x2d, w_s, b_s.T, row(sgu_norm_g), row(b_gate),
                w_out.astype(BF16), tm=min(512, n))

    kv = _memkv(mem, row(norm_mem_g), w_xkv.astype(BF16))
    w_r = jnp.pad(jnp.concatenate([w_rg, w_re], axis=1), ((0, 0), (0, LANES - MOE_GROUPS - N_EXPERTS)))
    b_r = jnp.pad(jnp.concatenate([b_rg, b_re]), (0, LANES - MOE_GROUPS - N_EXPERTS)).reshape(1, LANES)
    x2, xn2, logits = _xattn(x1.reshape(b, s, d), kv, row(norm_x_g), w_xq.astype(BF16),
                             w_xo.astype(BF16), row(norm_moe_g), w_r, b_r, tm=min(512, s))
    x2 = x2.reshape(n, d)
    xn2 = xn2.reshape(n, d)

    ids, gates, counts = _router(logits.reshape(n, LANES), tm=min(512, n))

    tb = ROW_BLOCK
    a_total = n * MOE_TOPK
    nb = a_total // tb + N_EXPERTS
    p_rows = nb * tb
    cnt = counts[0, MOE_GROUPS:MOE_GROUPS + N_EXPERTS].astype(I32)
    padded = (cnt + tb - 1) // tb * tb
    pends = jnp.cumsum(padded)
    pstarts = pends - padded
    expert = ids[:, 0:2]
    rank = ids[:, 2:4]
    dest = pstarts[expert] + rank
    slot_of = jnp.arange(n, dtype=I32)[:, None] + jnp.arange(MOE_TOPK, dtype=I32)[None, :] * n
    slots = jnp.zeros((p_rows,), I32).at[dest.reshape(-1)].set(slot_of.reshape(-1))
    block_start = jnp.arange(nb, dtype=I32) * tb
    block_expert = jnp.minimum(
        jnp.searchsorted(pends, block_start, side="right"), N_EXPERTS - 1).astype(I32)
    block_rows = jnp.clip(cnt[block_expert] - (block_start - pstarts[block_expert]), 0, tb)
    block_rows = jnp.where(block_start < pends[-1], block_rows, 0).astype(I32)
    nvalid = (pends[-1] // tb).astype(I32).reshape(1)

    y = _ffn(block_expert, block_rows, nvalid, slots.reshape(nb, tb), xn2, w1, w3, w2, a_total)
    return x2, y, gates


def kernel(x, mem, norm_mix_g, w_in, b_gate, b_if, conv_w, conv_b, mh_norm_g, sgu_norm_g, w_s, b_s, w_out, norm_x_g, norm_mem_g, w_xq, w_xkv, w_xo, norm_moe_g, w_rg, b_rg, w_re, b_re, w1, w3, w2, norm_f_g):
    assert w_in.shape[0] == 1, "the combine stage fuses the final norm: one layer only"
    b, s, d = x.shape
    n = b * s
    l = 0
    x2, y, gates = _layer(
        x, mem, norm_mix_g[l], w_in[l], b_gate[l], b_if[l], conv_w[l], conv_b[l], mh_norm_g[l],
        sgu_norm_g[l], w_s[l], b_s[l], w_out[l], norm_x_g[l], norm_mem_g[l], w_xq[l], w_xkv[l],
        w_xo[l], norm_moe_g[l], w_rg[l], b_rg[l], w_re[l], b_re[l], w1[l], w3[l], w2[l])
    return _combine(x2, y, gates, norm_f_g.reshape(1, d), tm=min(512, n)).reshape(b, s, d)
```

```python
import jax
import jax.numpy as jnp
from jax import lax
from jax.experimental import pallas as pl
from jax.experimental.pallas import tpu as pltpu

F32 = jnp.float32
BF16 = jnp.bfloat16
I32 = jnp.int32

EPS = 1e-6
LANES = 128
CHUNK = 128
MLSTM_HEADS = 4
SGU_GROUPS = 8
XATTN_HEADS = 4
MOE_GROUPS = 4
EXPERTS_PER_GROUP = 8
N_EXPERTS = MOE_GROUPS * EXPERTS_PER_GROUP
MOE_TOPK = 2
CONV_WIDTH = 4
CONV_HALO = 8
ROW_BLOCK = 256
ROW_BLOCK_BITS = ROW_BLOCK.bit_length() - 1
MOE_TILE = 256
VMEM_LIMIT = 56 * 1024 * 1024


def _sigmoid(x):
    return 1.0 / (1.0 + jnp.exp(-x))


def _rms(x, g):
    return x * lax.rsqrt(jnp.mean(x * x, axis=-1, keepdims=True) + EPS) * g


def _log_sigmoid(x):
    return jnp.minimum(x, 0.0) - jnp.log(1.0 + jnp.exp(-jnp.abs(x)))


def _gelu_tanh(x):
    c = 0.7978845608028654
    return 0.5 * x * (1.0 + jnp.tanh(c * (x + 0.044715 * (x * x * x))))


def _split3_dot(a, tri, dims):
    a_hi = a.astype(BF16)
    r1 = a - a_hi.astype(F32)
    a_mid = r1.astype(BF16)
    a_lo = (r1 - a_mid.astype(F32)).astype(BF16)
    if dims == "at":
        f = lambda p: jnp.dot(p, tri, preferred_element_type=F32)
    else:
        f = lambda p: jnp.dot(tri, p, preferred_element_type=F32)
    return f(a_hi) + f(a_mid) + f(a_lo)


def _split3_both(a, b):
    a_hi = a.astype(BF16)
    a_lo = (a - a_hi.astype(F32)).astype(BF16)
    b_hi = b.astype(BF16)
    b_lo = (b - b_hi.astype(F32)).astype(BF16)
    f = lambda p, r: jnp.dot(p, r, preferred_element_type=F32)
    return f(a_hi, b_hi) + (f(a_hi, b_lo) + f(a_lo, b_hi))


def _inproj_kernel(x_ref, g_ref, w_ref, wif_ref, wift_ref, bc_ref, br_ref,
                   proj_ref, gc_ref, gr_ref, xn_ref):
    @pl.when(pl.program_id(1) == 0)
    def _():
        xn = _rms(x_ref[...], g_ref[...]).astype(BF16)
        xn_ref[...] = xn
        gc_ref[...] = jnp.dot(xn, wif_ref[...], preferred_element_type=F32) + bc_ref[...]
        gr_ref[...] = lax.dot_general(wift_ref[...], xn, (((1,), (1,)), ((), ())),
                                      preferred_element_type=F32) + br_ref[...]

    proj_ref[...] = jnp.dot(xn_ref[...], w_ref[...], preferred_element_type=F32).astype(BF16)


def _inproj(x2d, g, w_main, w_if, w_ift, b_col, b_row, tm, tn):
    n, d = x2d.shape
    nc = w_main.shape[1]
    return pl.pallas_call(
        _inproj_kernel,
        grid=(n // tm, nc // tn),
        in_specs=[
            pl.BlockSpec((tm, d), lambda i, j: (i, 0)),
            pl.BlockSpec((1, d), lambda i, j: (0, 0)),
            pl.BlockSpec((d, tn), lambda i, j: (0, j)),
            pl.BlockSpec((d, LANES), lambda i, j: (0, 0)),
            pl.BlockSpec((8, d), lambda i, j: (0, 0)),
            pl.BlockSpec((1, LANES), lambda i, j: (0, 0)),
            pl.BlockSpec((8, 1), lambda i, j: (0, 0)),
        ],
        out_specs=[
            pl.BlockSpec((tm, tn), lambda i, j: (i, j)),
            pl.BlockSpec((tm, LANES), lambda i, j: (i, 0)),
            pl.BlockSpec((8, tm), lambda i, j: (0, i)),
        ],
        out_shape=[
            jax.ShapeDtypeStruct((n, nc), BF16),
            jax.ShapeDtypeStruct((n, LANES), F32),
            jax.ShapeDtypeStruct((8, n), F32),
        ],
        scratch_shapes=[pltpu.VMEM((tm, d), BF16)],
        compiler_params=pltpu.CompilerParams(
            dimension_semantics=("arbitrary", "arbitrary"), vmem_limit_bytes=VMEM_LIMIT),
        name="inproj",
    )(x2d, g, w_main, w_if, w_ift, b_col, b_row)


def _mlstm_kernel(qk_ref, v_ref, o_ref, gc_ref, gr_ref, cw_ref, cb_ref, mhg_ref,
                  ya_ref, xbuf, c_ref, n_ref, m_ref):
    L = CHUNK
    H = MLSTM_HEADS
    dqk = qk_ref.shape[-1]
    dh = dqk // (2 * H)

    @pl.when(pl.program_id(1) == 0)
    def _():
        xbuf[0:CONV_HALO, :] = jnp.zeros((CONV_HALO, dqk), F32)
        c_ref[...] = jnp.zeros_like(c_ref)
        n_ref[...] = jnp.zeros_like(n_ref)
        m_ref[...] = jnp.zeros_like(m_ref)

    xbuf[CONV_HALO:CONV_HALO + L, :] = qk_ref[0].astype(F32)
    acc = jnp.broadcast_to(cb_ref[...], (L, dqk))
    for j in range(CONV_WIDTH):
        off = CONV_HALO + j - (CONV_WIDTH - 1)
        acc = acc + cw_ref[j:j + 1, :] * xbuf[off:off + L, :]
    act = acc * _sigmoid(acc)
    xbuf[0:CONV_HALO, :] = xbuf[L:L + CONV_HALO, :]

    gc = gc_ref[0]
    gr = gr_ref[...]
    row = lax.broadcasted_iota(I32, (L, L), 0)
    col = lax.broadcasted_iota(I32, (L, L), 1)
    causal = row >= col
    tril = causal.astype(BF16)
    triu = (row <= col).astype(BF16)
    b_c = _split3_dot(_log_sigmoid(gc), tril, "ta")
    b_r = _split3_dot(_log_sigmoid(gr), triu, "at")

    for h in range(H):
        q = act[:, h * dh:(h + 1) * dh]
        k = act[:, (H + h) * dh:(H + h + 1) * dh] * (dh ** -0.5)
        v = v_ref[0, :, h * dh:(h + 1) * dh]
        bcol = b_c[:, H + h:H + h + 1]
        icol = gc[:, h:h + 1]
        brow = b_r[H + h:H + h + 1, :]
        irow = gr[h:h + 1, :]
        m_prev = m_ref[h:h + 1, 0:1]

        dmat = jnp.where(causal, bcol - brow + irow, -jnp.inf)
        inter = bcol + m_prev
        m_t = jnp.maximum(inter, jnp.max(dmat, axis=-1, keepdims=True))
        dexp = jnp.exp(dmat - m_t)
        inter_w = jnp.exp(inter - m_t)
        qb = q.astype(BF16)
        kb = k.astype(BF16)
        s = lax.dot_general(qb, kb, (((1,), (1,)), ((), ())), preferred_element_type=F32) * dexp
        c_old = c_ref[h]
        n_old = n_ref[h:h + 1, :]
        num = (jnp.dot(s.astype(BF16), v, preferred_element_type=F32)
               + inter_w * jnp.dot(qb, c_old.astype(BF16), preferred_element_type=F32))
        den = (jnp.sum(s, axis=-1, keepdims=True)
               + inter_w * jnp.sum(q * n_old, axis=-1, keepdims=True))
        hh = num * (1.0 / jnp.maximum(jnp.abs(den), jnp.exp(-m_t)))

        b_last = bcol[L - 1:L, :]
        wlog = b_last - bcol + icol
        m_new = jnp.maximum(b_last + m_prev, jnp.max(wlog, axis=0, keepdims=True))
        decay = jnp.exp(b_last + m_prev - m_new)
        kw = k * jnp.exp(wlog - m_new)
        c_ref[h] = decay * c_old + jnp.dot(kw.T.astype(BF16), v, preferred_element_type=F32)
        n_ref[h:h + 1, :] = decay * n_old + jnp.sum(kw, axis=0, keepdims=True)
        m_ref[h:h + 1, :] = jnp.broadcast_to(m_new, (1, LANES))

        y = _rms(hh, mhg_ref[:, h * dh:(h + 1) * dh])
        og = _sigmoid(o_ref[0, :, h * dh:(h + 1) * dh].astype(F32))
        ya_ref[0, :, h * dh:(h + 1) * dh] = (y * og).astype(BF16)


def _mlstm(proj3, gc3, gr, conv_w, conv_b, mh_g, d):
    b, s, _ = proj3.shape
    nc = s // CHUNK
    dqk = 2 * d
    dh = d // MLSTM_HEADS
    return pl.pallas_call(
        _mlstm_kernel,
        grid=(b, nc),
        in_specs=[
            pl.BlockSpec((1, CHUNK, dqk), lambda bi, ci: (bi, ci, 0)),
            pl.BlockSpec((1, CHUNK, d), lambda bi, ci: (bi, ci, 2)),
            pl.BlockSpec((1, CHUNK, d), lambda bi, ci: (bi, ci, 3)),
            pl.BlockSpec((1, CHUNK, LANES), lambda bi, ci: (bi, ci, 0)),
            pl.BlockSpec((8, CHUNK), lambda bi, ci: (0, bi * nc + ci)),
            pl.BlockSpec((CONV_WIDTH, dqk), lambda bi, ci: (0, 0)),
            pl.BlockSpec((1, dqk), lambda bi, ci: (0, 0)),
            pl.BlockSpec((1, d), lambda bi, ci: (0, 0)),
        ],
        out_specs=pl.BlockSpec((1, CHUNK, d), lambda bi, ci: (bi, ci, 0)),
        out_shape=jax.ShapeDtypeStruct((b, s, d), BF16),
        scratch_shapes=[
            pltpu.VMEM((CHUNK + CONV_HALO, dqk), F32),
            pltpu.VMEM((MLSTM_HEADS, dh, dh), F32),
            pltpu.VMEM((8, dh), F32),
            pltpu.VMEM((8, LANES), F32),
        ],
        compiler_params=pltpu.CompilerParams(
            dimension_semantics=("arbitrary", "arbitrary"), vmem_limit_bytes=VMEM_LIMIT),
        name="mlstm",
    )(proj3, proj3, proj3, gc3, gr, conv_w, conv_b, mh_g)


def _merge_kernel(u_ref, sv_ref, ga_ref, gb_ref, ya_ref, x_ref, ws_ref, bst_ref, sg_ref, bg_ref,
                  wout_ref, x1_ref, yb_ref):
    tm, d = x_ref.shape
    L = CHUNK
    G = SGU_GROUPS
    gd = d // G
    uu = _gelu_tanh(u_ref[...].astype(F32))
    vn = _rms(_gelu_tanh(sv_ref[...].astype(F32)), sg_ref[...]).astype(BF16)
    row = lax.broadcasted_iota(I32, (L, L), 0)
    col = lax.broadcasted_iota(I32, (L, L), 1)
    causal = row >= col
    for g in range(G):
        wg = jnp.where(causal, ws_ref[g], 0.0).astype(BF16)
        bias = bst_ref[:, g:g + 1]
        for ci in range(tm // L):
            mixed = jnp.dot(wg, vn[ci * L:(ci + 1) * L, g * gd:(g + 1) * gd],
                            preferred_element_type=F32)
            yb_ref[ci * L:(ci + 1) * L, g * gd:(g + 1) * gd] = (
                uu[ci * L:(ci + 1) * L, g * gd:(g + 1) * gd] * (mixed + bias))
    g_a = _sigmoid(ga_ref[...].astype(F32) + bg_ref[:, 0:d])
    g_b = _sigmoid(gb_ref[...].astype(F32) + bg_ref[:, d:2 * d])
    merged = (g_a * ya_ref[...].astype(F32) + g_b * yb_ref[...]).astype(BF16)
    x1_ref[...] = x_ref[...] + jnp.dot(merged, wout_ref[...], preferred_element_type=F32)


def _merge(proj, ya, x2d, w_s, b_st, sgu_g, b_gate, w_out, tm):
    n, d = x2d.shape
    full = lambda shape: pl.BlockSpec(shape, lambda i: (0,) * len(shape))
    return pl.pallas_call(
        _merge_kernel,
        grid=(n // tm,),
        in_specs=[
            pl.BlockSpec((tm, d), lambda i: (i, 4)),
            pl.BlockSpec((tm, d), lambda i: (i, 5)),
            pl.BlockSpec((tm, d), lambda i: (i, 6)),
            pl.BlockSpec((tm, d), lambda i: (i, 7)),
            pl.BlockSpec((tm, d), lambda i: (i, 0)),
            pl.BlockSpec((tm, d), lambda i: (i, 0)),
            full(w_s.shape), full(b_st.shape), full(sgu_g.shape), full(b_gate.shape), full(w_out.shape),
        ],
        out_specs=pl.BlockSpec((tm, d), lambda i: (i, 0)),
        out_shape=jax.ShapeDtypeStruct((n, d), F32),
        scratch_shapes=[pltpu.VMEM((tm, d), F32)],
        compiler_params=pltpu.CompilerParams(
            dimension_semantics=("arbitrary",), vmem_limit_bytes=VMEM_LIMIT),
        name="merge",
    )(proj, proj, proj, proj, ya, x2d, w_s, b_st, sgu_g, b_gate, w_out)


def _memkv_kernel(mem_ref, g_ref, w_ref, kv_ref):
    mn = _rms(mem_ref[0], g_ref[...]).astype(BF16)
    kv_ref[0] = jnp.dot(mn, w_ref[...], preferred_element_type=F32).astype(BF16)


def _memkv(mem, g, w_xkv):
    b, m, d = mem.shape
    dk = w_xkv.shape[1]
    return pl.pallas_call(
        _memkv_kernel,
        grid=(b,),
        in_specs=[
            pl.BlockSpec((1, m, d), lambda i: (i, 0, 0)),
            pl.BlockSpec((1, d), lambda i: (0, 0)),
            pl.BlockSpec((d, dk), lambda i: (0, 0)),
        ],
        out_specs=pl.BlockSpec((1, m, dk), lambda i: (i, 0, 0)),
        out_shape=jax.ShapeDtypeStruct((b, m, dk), BF16),
        compiler_params=pltpu.CompilerParams(
            dimension_semantics=("arbitrary",), vmem_limit_bytes=VMEM_LIMIT),
        name="memkv",
    )(mem, g, w_xkv)


def _xattn_kernel(x1_ref, kv_ref, gx_ref, wq_ref, wo_ref, gm_ref, wr_ref, br_ref,
                  x2_ref, xn2_ref, lg_ref, att_ref):
    tm, d = x1_ref.shape[1:]
    H = XATTN_HEADS
    dh = d // H
    x1 = x1_ref[0]
    xn = _rms(x1, gx_ref[...]).astype(BF16)
    q = (jnp.dot(xn, wq_ref[...], preferred_element_type=F32) * (dh ** -0.5)).astype(BF16)
    for h in range(H):
        kh = kv_ref[0, :, h * dh:(h + 1) * dh]
        vh = kv_ref[0, :, d + h * dh:d + (h + 1) * dh]
        s = lax.dot_general(q[:, h * dh:(h + 1) * dh], kh, (((1,), (1,)), ((), ())),
                            preferred_element_type=F32)
        e = jnp.exp(s - jnp.max(s, axis=-1, keepdims=True))
        p = e * (1.0 / jnp.sum(e, axis=-1, keepdims=True))
        att_ref[:, h * dh:(h + 1) * dh] = jnp.dot(p.astype(BF16), vh,
                                                 preferred_element_type=F32).astype(BF16)
    x2 = x1 + jnp.dot(att_ref[...], wo_ref[...], preferred_element_type=F32)
    x2_ref[0] = x2
    xn2 = _rms(x2, gm_ref[...])
    xn2_ref[0] = xn2
    lg_ref[0] = _split3_both(xn2, wr_ref[...]) + br_ref[...]


def _xattn(x1_3d, kv, gx, w_xq, w_xo, gm, w_r, b_r, tm):
    b, s, d = x1_3d.shape
    m = kv.shape[1]
    full = lambda shape: pl.BlockSpec(shape, lambda bi, i: (0,) * len(shape))
    return pl.pallas_call(
        _xattn_kernel,
        grid=(b, s // tm),
        in_specs=[
            pl.BlockSpec((1, tm, d), lambda bi, i: (bi, i, 0)),
            pl.BlockSpec((1, m, 2 * d), lambda bi, i: (bi, 0, 0)),
            full(gx.shape), full(w_xq.shape), full(w_xo.shape), full(gm.shape), full(w_r.shape),
            full(b_r.shape),
        ],
        out_specs=[
            pl.BlockSpec((1, tm, d), lambda bi, i: (bi, i, 0)),
            pl.BlockSpec((1, tm, d), lambda bi, i: (bi, i, 0)),
            pl.BlockSpec((1, tm, LANES), lambda bi, i: (bi, i, 0)),
        ],
        out_shape=[
            jax.ShapeDtypeStruct((b, s, d), F32),
            jax.ShapeDtypeStruct((b, s, d), F32),
            jax.ShapeDtypeStruct((b, s, LANES), F32),
        ],
        scratch_shapes=[pltpu.VMEM((tm, d), BF16)],
        compiler_params=pltpu.CompilerParams(
            dimension_semantics=("arbitrary", "arbitrary"), vmem_limit_bytes=VMEM_LIMIT),
        name="xattn",
    )(x1_3d, kv, gx, w_xq, w_xo, gm, w_r, b_r)


def _router_kernel(lg_ref, ids_ref, gate_ref, cnt_ref, carry_ref):
    tm = lg_ref.shape[0]
    G, E = MOE_GROUPS, N_EXPERTS

    @pl.when(pl.program_id(0) == 0)
    def _():
        carry_ref[...] = jnp.zeros_like(carry_ref)

    lg = lg_ref[...]
    lane = lax.broadcasted_iota(I32, (tm, LANES), 1)
    lane_f = lane.astype(F32)
    ninf = -jnp.inf

    def first_lane(mask):
        return jnp.min(jnp.where(mask, lane_f, float(LANES)), axis=-1, keepdims=True).astype(I32)

    glm = jnp.where(lane < G, lg, ninf)
    gmax = jnp.max(glm, axis=-1, keepdims=True)
    g_idx = first_lane(glm == gmax)
    p_g = 1.0 / jnp.sum(jnp.exp(glm - gmax), axis=-1, keepdims=True)

    lo = G + g_idx * EXPERTS_PER_GROUP
    em = jnp.where((lane >= lo) & (lane < lo + EXPERTS_PER_GROUP), lg, ninf)
    top1 = jnp.max(em, axis=-1, keepdims=True)
    i1 = first_lane(em == top1)
    em2 = jnp.where(lane == i1, ninf, em)
    top2 = jnp.max(em2, axis=-1, keepdims=True)
    i2 = first_lane(em2 == top2)
    e2 = jnp.exp(top2 - top1)
    inv = 1.0 / (1.0 + e2)
    gate1 = p_g * inv
    gate2 = p_g * (e2 * inv)

    hot1 = lane == i1
    hot2 = lane == i2
    onehot = (hot1 | hot2).astype(BF16)
    r = lax.broadcasted_iota(I32, (tm, tm), 0)
    c = lax.broadcasted_iota(I32, (tm, tm), 1)
    strict = (r > c).astype(BF16)
    before = jnp.dot(strict, onehot, preferred_element_type=F32) + carry_ref[...]
    rank1 = jnp.sum(jnp.where(hot1, before, 0.0), axis=-1, keepdims=True)
    rank2 = jnp.sum(jnp.where(hot2, before, 0.0), axis=-1, keepdims=True)
    carry_ref[...] = carry_ref[...] + jnp.sum(onehot.astype(F32), axis=0, keepdims=True)

    ids = jnp.where(lane == 0, i1, 0)
    ids = jnp.where(lane == 1, i2, ids)
    ids = jnp.where(lane == 2, rank1.astype(I32), ids)
    ids = jnp.where(lane == 3, rank2.astype(I32), ids)
    ids_ref[...] = ids
    gate_ref[...] = jnp.where(lane == 0, gate1, jnp.where(lane == 1, gate2, 0.0))
    cnt_ref[...] = carry_ref[...]


def _router(logits, tm):
    n = logits.shape[0]
    return pl.pallas_call(
        _router_kernel,
        grid=(n // tm,),
        in_specs=[pl.BlockSpec((tm, LANES), lambda i: (i, 0))],
        out_specs=[
            pl.BlockSpec((tm, LANES), lambda i: (i, 0)),
            pl.BlockSpec((tm, LANES), lambda i: (i, 0)),
            pl.BlockSpec((1, LANES), lambda i: (0, 0)),
        ],
        out_shape=[
            jax.ShapeDtypeStruct((n, LANES), I32),
            jax.ShapeDtypeStruct((n, LANES), F32),
            jax.ShapeDtypeStruct((1, LANES), F32),
        ],
        scratch_shapes=[pltpu.VMEM((1, LANES), F32)],
        compiler_params=pltpu.CompilerParams(dimension_semantics=("arbitrary",)),
        name="router",
    )(logits)


def _plan_kernel(ids_ref, cnt_ref, dest_ref):
    tm = ids_ref.shape[0]
    ids = ids_ref[...]
    cnt = cnt_ref[...].astype(I32)
    padded = ((cnt + (ROW_BLOCK - 1)) >> ROW_BLOCK_BITS) << ROW_BLOCK_BITS
    k = lax.broadcasted_iota(I32, (LANES, LANES), 0)
    l = lax.broadcasted_iota(I32, (LANES, LANES), 1)
    before = (k < l).astype(BF16)
    pstart = _split3_dot(jnp.broadcast_to(padded.astype(F32), (8, LANES)), before, "at")[0:1]
    lane = lax.broadcasted_iota(I32, (tm, LANES), 1)

    def dest(col):
        start = jnp.sum(jnp.where(lane == ids[:, col:col + 1], pstart, 0.0), axis=-1, keepdims=True)
        return start + ids[:, col + 2:col + 3].astype(F32)

    d = jnp.where(lane == 0, dest(0), jnp.where(lane == 1, dest(1), 0.0))
    dest_ref[...] = d.T[0:8, :].astype(I32)


def _plan(ids, counts, tm):
    n = ids.shape[0]
    return pl.pallas_call(
        _plan_kernel,
        grid=(n // tm,),
        in_specs=[pl.BlockSpec((tm, LANES), lambda i: (i, 0)),
                  pl.BlockSpec((1, LANES), lambda i: (0, 0))],
        out_specs=pl.BlockSpec((8, tm), lambda i: (0, i)),
        out_shape=jax.ShapeDtypeStruct((8, n), I32),
        compiler_params=pltpu.CompilerParams(dimension_semantics=("arbitrary",)),
        name="plan",
    )(ids, counts)


def _dispatch_kernel(padstart_ref, padlen_ref, nvalid_ref, dest_hbm, x_ref, xs_hbm,
                     dsm, xcopy, zbuf, dsem, ssem, zsem):
    i = pl.program_id(0)
    nt = pl.num_programs(0)
    tm = x_ref.shape[0]
    slot = i % 2
    nb = xs_hbm.shape[0] // ROW_BLOCK

    def dest_copy(tile, k):
        buf = tile % 3
        return pltpu.make_async_copy(dest_hbm.at[k, pl.ds(tile * tm, tm)], dsm.at[buf, k],
                                     dsem.at[buf, k])

    def row_copy(r, k):
        return pltpu.make_async_copy(xcopy.at[slot, pl.ds(r, 1), :],
                                     xs_hbm.at[pl.ds(dsm[i % 3, k, r], 1), :], ssem.at[slot])

    def wait_rows(buf):
        for _ in range(MOE_TOPK):
            pltpu.make_async_copy(xcopy.at[buf], xs_hbm.at[pl.ds(0, tm), :], ssem.at[buf]).wait()

    def zero_copies(e):
        start = padstart_ref[e]
        n_pad = padlen_ref[e]
        head = jnp.minimum((-start) & 7, n_pad)
        for t in range(7):
            yield t < head, pltpu.make_async_copy(
                zbuf.at[pl.ds(0, 1), :], xs_hbm.at[pl.ds(start + t, 1), :], zsem)
        rest = n_pad - head
        for bit in range(3, ROW_BLOCK_BITS):
            size = 1 << bit
            off = pl.multiple_of(start + head + (rest & (size - 1)), 8)
            yield ((rest >> bit) & 1) == 1, pltpu.make_async_copy(
                zbuf.at[pl.ds(0, size), :], xs_hbm.at[pl.ds(off, size), :], zsem)

    def unused_block_copies():
        for blk in range(nb - N_EXPERTS, nb):
            yield blk >= nvalid_ref[0], pltpu.make_async_copy(
                zbuf, xs_hbm.at[pl.ds(blk * ROW_BLOCK, ROW_BLOCK), :], zsem)

    @pl.when(i == 0)
    def _():
        zbuf[...] = jnp.zeros_like(zbuf)
        for flag, cp in unused_block_copies():
            pl.when(flag)(cp.start)
        for e in range(N_EXPERTS):
            for flag, cp in zero_copies(e):
                pl.when(flag)(cp.start)
        for k in range(MOE_TOPK):
            dest_copy(0, k).start()

            @pl.when(nt > 1)
            def _():
                dest_copy(1, k).start()

    for k in range(MOE_TOPK):
        dest_copy(i, k).wait()

        @pl.when(i + 2 < nt)
        def _():
            dest_copy(i + 2, k).start()

    xcopy[slot] = x_ref[...]
    for r in range(tm):
        for k in range(MOE_TOPK):
            row_copy(r, k).start()

    @pl.when(i >= 1)
    def _():
        wait_rows(1 - slot)

    @pl.when(i == nt - 1)
    def _():
        wait_rows(slot)
        for flag, cp in unused_block_copies():
            pl.when(flag)(cp.wait)
        for e in range(N_EXPERTS):
            for flag, cp in zero_copies(e):
                pl.when(flag)(cp.wait)


def _dispatch(pad_start, pad_len, nvalid, dest_t, xn2, p_rows):
    n, d = xn2.shape
    tm = MOE_TILE
    grid_spec = pltpu.PrefetchScalarGridSpec(
        num_scalar_prefetch=3,
        grid=(n // tm,),
        in_specs=[
            pl.BlockSpec(memory_space=pl.ANY),
            pl.BlockSpec((tm, d), lambda i, ps, pn, nv: (i, 0)),
        ],
        out_specs=pl.BlockSpec(memory_space=pl.ANY),
        scratch_shapes=[
            pltpu.SMEM((3, MOE_TOPK, tm), I32),
            pltpu.VMEM((2, tm, d), F32),
            pltpu.VMEM((ROW_BLOCK, d), F32),
            pltpu.SemaphoreType.DMA((3, MOE_TOPK)),
            pltpu.SemaphoreType.DMA((2,)),
            pltpu.SemaphoreType.DMA,
        ],
    )
    return pl.pallas_call(
        _dispatch_kernel,
        grid_spec=grid_spec,
        out_shape=jax.ShapeDtypeStruct((p_rows, d), F32),
        compiler_params=pltpu.CompilerParams(
            dimension_semantics=("arbitrary",), vmem_limit_bytes=VMEM_LIMIT),
        name="dispatch",
    )(pad_start, pad_len, nvalid, dest_t, xn2)


def _ffn_kernel(bexp_ref, nvalid_ref, xs_ref, w1_ref, w3_ref, w2_ref, o_ref, w1b, w3b, w2b):
    i = pl.program_id(0)
    nvalid = nvalid_ref[0]

    @pl.when(i < nvalid)
    def _():
        changed = jnp.logical_or(i == 0, bexp_ref[i] != bexp_ref[jnp.maximum(i - 1, 0)])

        @pl.when(changed)
        def _():
            w1b[...] = w1_ref[0].astype(BF16)
            w3b[...] = w3_ref[0].astype(BF16)
            w2b[...] = w2_ref[0].astype(BF16)

        xb = xs_ref[...].astype(BF16)
        a = jnp.dot(xb, w1b[...], preferred_element_type=F32)
        g = jnp.dot(xb, w3b[...], preferred_element_type=F32)
        hb = (a * _sigmoid(a) * g).astype(BF16)
        o_ref[...] = jnp.dot(hb, w2b[...], preferred_element_type=F32)

    @pl.when(i >= nvalid)
    def _():
        o_ref[...] = jnp.zeros_like(o_ref)


def _ffn(block_expert, nvalid, xs, w1, w3, w2):
    p_rows, d = xs.shape
    tb = ROW_BLOCK
    _, _, f = w1.shape
    grid_spec = pltpu.PrefetchScalarGridSpec(
        num_scalar_prefetch=2,
        grid=(p_rows // tb,),
        in_specs=[
            pl.BlockSpec((tb, d), lambda i, be, nv: (jnp.minimum(i, nv[0] - 1), 0)),
            pl.BlockSpec((1, d, f), lambda i, be, nv: (be[i], 0, 0)),
            pl.BlockSpec((1, d, f), lambda i, be, nv: (be[i], 0, 0)),
            pl.BlockSpec((1, f, d), lambda i, be, nv: (be[i], 0, 0)),
        ],
        out_specs=pl.BlockSpec((tb, d), lambda i, be, nv: (i, 0)),
        scratch_shapes=[
            pltpu.VMEM((d, f), BF16),
            pltpu.VMEM((d, f), BF16),
            pltpu.VMEM((f, d), BF16),
        ],
    )
    return pl.pallas_call(
        _ffn_kernel,
        grid_spec=grid_spec,
        out_shape=jax.ShapeDtypeStruct((p_rows, d), F32),
        compiler_params=pltpu.CompilerParams(
            dimension_semantics=("arbitrary",), vmem_limit_bytes=VMEM_LIMIT),
        name="ffn",
    )(block_expert, nvalid, xs, w1, w3, w2)


def _combine_kernel(dest_hbm, y_hbm, x2_ref, gate_ref, g_ref, out_ref, dsm, ybuf, dsem, gsem):
    j = pl.program_id(0)
    nt = pl.num_programs(0) - 1
    tm = x2_ref.shape[0]

    def dest_copy(tile, k):
        buf = tile % 3
        return pltpu.make_async_copy(dest_hbm.at[k, pl.ds(tile * tm, tm)], dsm.at[buf, k],
                                     dsem.at[buf, k])

    def row_copy(r, k):
        return pltpu.make_async_copy(y_hbm.at[pl.ds(dsm[j % 3, k, r], 1), :],
                                     ybuf.at[j % 2, k, pl.ds(r, 1), :], gsem.at[j % 2])

    @pl.when(j == 0)
    def _():
        for k in range(MOE_TOPK):
            dest_copy(0, k).start()

    @pl.when(j < nt)
    def _():
        for k in range(MOE_TOPK):
            dest_copy(j, k).wait()

            @pl.when(j + 1 < nt)
            def _():
                dest_copy(j + 1, k).start()

        for r in range(tm):
            for k in range(MOE_TOPK):
                row_copy(r, k).start()

    @pl.when(j >= 1)
    def _():
        buf = (j - 1) % 2
        for k in range(MOE_TOPK):
            pltpu.make_async_copy(y_hbm.at[pl.ds(0, tm), :], ybuf.at[buf, k], gsem.at[buf]).wait()
        gate = gate_ref[...]
        x = x2_ref[...] + gate[:, 0:1] * ybuf[buf, 0] + gate[:, 1:2] * ybuf[buf, 1]
        out_ref[...] = _rms(x, g_ref[...])


def _combine(dest_t, y, x2, gates, g):
    n, d = x2.shape
    tm = MOE_TILE
    nt = n // tm
    prev = lambda j: (jnp.maximum(j - 1, 0), 0)
    return pl.pallas_call(
        _combine_kernel,
        grid=(nt + 1,),
        in_specs=[
            pl.BlockSpec(memory_space=pl.ANY),
            pl.BlockSpec(memory_space=pl.ANY),
            pl.BlockSpec((tm, d), prev),
            pl.BlockSpec((tm, LANES), prev),
            pl.BlockSpec((1, d), lambda j: (0, 0)),
        ],
        out_specs=pl.BlockSpec((tm, d), prev),
        out_shape=jax.ShapeDtypeStruct((n, d), F32),
        scratch_shapes=[
            pltpu.SMEM((3, MOE_TOPK, tm), I32),
            pltpu.VMEM((2, MOE_TOPK, tm, d), F32),
            pltpu.SemaphoreType.DMA((3, MOE_TOPK)),
            pltpu.SemaphoreType.DMA((2,)),
        ],
        compiler_params=pltpu.CompilerParams(
            dimension_semantics=("arbitrary",), vmem_limit_bytes=VMEM_LIMIT),
        name="combine",
    )(dest_t, y, x2, gates, g)


def kernel(x, mem, norm_mix_g, w_in, b_gate, b_if, conv_w, conv_b, mh_norm_g, sgu_norm_g, w_s, b_s, w_out, norm_x_g, norm_mem_g, w_xq, w_xkv, w_xo, norm_moe_g, w_rg, b_rg, w_re, b_re, w1, w3, w2, norm_f_g):
    assert w_in.shape[0] == 1, "the combine stage fuses the final norm: one layer only"
    (norm_mix_g, w_in, b_gate, b_if, conv_w, conv_b, mh_norm_g, sgu_norm_g, w_s, b_s, w_out, norm_x_g,
     norm_mem_g, w_xq, w_xkv, w_xo, norm_moe_g, w_rg, b_rg, w_re, b_re, w1, w3, w2) = (
        p[0] for p in (norm_mix_g, w_in, b_gate, b_if, conv_w, conv_b, mh_norm_g, sgu_norm_g, w_s, b_s,
                       w_out, norm_x_g, norm_mem_g, w_xq, w_xkv, w_xo, norm_moe_g, w_rg, b_rg, w_re,
                       b_re, w1, w3, w2))
    b, s, d = x.shape
    n = b * s
    H = MLSTM_HEADS
    x2d = x.reshape(n, d)
    row = lambda v: v.reshape(1, -1)

    c_if = 4 * d
    w_main = jnp.concatenate([w_in[:, :c_if], w_in[:, c_if + 2 * H:]], axis=1).astype(BF16)
    w_if = w_in[:, c_if:c_if + 2 * H]
    w_if_col = jnp.pad(w_if, ((0, 0), (0, LANES - 2 * H))).astype(BF16)
    w_if_row = w_if.T.astype(BF16)
    b_col = jnp.pad(b_if, (0, LANES - 2 * H)).reshape(1, LANES)
    b_row = b_if.reshape(2 * H, 1)

    proj, gc, gr = _inproj(x2d, row(norm_mix_g), w_main, w_if_col, w_if_row, b_col, b_row,
                           tm=min(1024, n), tn=1024)
    ya = _mlstm(proj.reshape(b, s, -1), gc.reshape(b, s, LANES), gr, conv_w, row(conv_b),
                row(mh_norm_g), d)
    x1 = _merge(proj, ya.reshape(n, d), x2d, w_s, b_s.T, row(sgu_norm_g), row(b_gate),
                w_out.astype(BF16), tm=min(512, n))

    kv = _memkv(mem, row(norm_mem_g), w_xkv.astype(BF16))
    n_route = MOE_GROUPS + N_EXPERTS
    w_r = jnp.pad(jnp.concatenate([w_rg, w_re], axis=1), ((0, 0), (0, LANES - n_route)))
    b_r = jnp.pad(jnp.concatenate([b_rg, b_re]), (0, LANES - n_route)).reshape(1, LANES)
    x2, xn2, logits = _xattn(x1.reshape(b, s, d), kv, row(norm_x_g), w_xq.astype(BF16),
                             w_xo.astype(BF16), row(norm_moe_g), w_r, b_r, tm=min(512, s))
    x2 = x2.reshape(n, d)
    xn2 = xn2.reshape(n, d)

    ids, gates, counts = _router(logits.reshape(n, LANES), tm=min(512, n))
    dest_t = _plan(ids, counts, tm=min(512, n))

    tb = ROW_BLOCK
    nb = n * MOE_TOPK // tb + N_EXPERTS
    cnt = counts[0, MOE_GROUPS:n_route].astype(I32)
    padded = (cnt + tb - 1) // tb * tb
    pends = jnp.cumsum(padded)
    block_start = jnp.arange(nb, dtype=I32) * tb
    block_expert = jnp.minimum(jnp.sum(pends[None, :] <= block_start[:, None], axis=1),
                               N_EXPERTS - 1).astype(I32)
    nvalid = (pends[-1] // tb).astype(I32).reshape(1)
    pad_start = (pends - padded + cnt).astype(I32)
    pad_len = (padded - cnt).astype(I32)

    xs = _dispatch(pad_start, pad_len, nvalid, dest_t, xn2, nb * tb)
    y = _ffn(block_expert, nvalid, xs, w1, w3, w2)
    return _combine(dest_t, y, x2, gates, row(norm_f_g)).reshape(b, s, d)
```

```python
import jax
import jax.numpy as jnp
from jax import lax
from jax.experimental import pallas as pl
from jax.experimental.pallas import tpu as pltpu

F32 = jnp.float32
BF16 = jnp.bfloat16
I32 = jnp.int32

EPS = 1e-6
LANES = 128
CHUNK = 128
MLSTM_HEADS = 4
SGU_GROUPS = 8
XATTN_HEADS = 4
MOE_GROUPS = 4
EXPERTS_PER_GROUP = 8
N_EXPERTS = MOE_GROUPS * EXPERTS_PER_GROUP
MOE_TOPK = 2
CONV_WIDTH = 4
CONV_HALO = 8
ROW_BLOCK = 256
ROW_BLOCK_BITS = ROW_BLOCK.bit_length() - 1
MOE_TILE = 256
VMEM_LIMIT = 56 * 1024 * 1024


def _sigmoid(x):
    return 0.5 * jnp.tanh(0.5 * x) + 0.5


def _rms(x, g):
    return x * lax.rsqrt(jnp.mean(x * x, axis=-1, keepdims=True) + EPS) * g


def _log_sigmoid(x):
    return jnp.minimum(x, 0.0) - jnp.log(1.0 + jnp.exp(-jnp.abs(x)))


def _gelu_tanh(x):
    c = 0.7978845608028654
    return 0.5 * x * (1.0 + jnp.tanh(c * (x + 0.044715 * (x * x * x))))


def _split3_dot(a, tri, dims):
    a_hi = a.astype(BF16)
    r1 = a - a_hi.astype(F32)
    a_mid = r1.astype(BF16)
    a_lo = (r1 - a_mid.astype(F32)).astype(BF16)
    if dims == "at":
        f = lambda p: jnp.dot(p, tri, preferred_element_type=F32)
    else:
        f = lambda p: jnp.dot(tri, p, preferred_element_type=F32)
    return f(a_hi) + f(a_mid) + f(a_lo)


def _split3_both(a, b):
    a_hi = a.astype(BF16)
    a_lo = (a - a_hi.astype(F32)).astype(BF16)
    b_hi = b.astype(BF16)
    b_lo = (b - b_hi.astype(F32)).astype(BF16)
    f = lambda p, r: jnp.dot(p, r, preferred_element_type=F32)
    return f(a_hi, b_hi) + (f(a_hi, b_lo) + f(a_lo, b_hi))


def _inproj_kernel(x_ref, g_ref, w_ref, wif_ref, bc_ref, proj_ref, gc_ref):
    xn = _rms(x_ref[...], g_ref[...]).astype(BF16)
    proj_ref[...] = jnp.dot(xn, w_ref[...], preferred_element_type=F32).astype(BF16)

    @pl.when(pl.program_id(1) == 0)
    def _():
        gc_ref[...] = jnp.dot(xn, wif_ref[...], preferred_element_type=F32) + bc_ref[...]


def _inproj(x2d, g, w_main, w_if, b_col, tm, tn):
    n, d = x2d.shape
    nc = w_main.shape[1]
    return pl.pallas_call(
        _inproj_kernel,
        grid=(n // tm, nc // tn),
        in_specs=[
            pl.BlockSpec((tm, d), lambda i, j: (i, 0)),
            pl.BlockSpec((1, d), lambda i, j: (0, 0)),
            pl.BlockSpec((d, tn), lambda i, j: (0, j)),
            pl.BlockSpec((d, LANES), lambda i, j: (0, 0)),
            pl.BlockSpec((1, LANES), lambda i, j: (0, 0)),
        ],
        out_specs=[
            pl.BlockSpec((tm, tn), lambda i, j: (i, j)),
            pl.BlockSpec((tm, LANES), lambda i, j: (i, 0)),
        ],
        out_shape=[
            jax.ShapeDtypeStruct((n, nc), BF16),
            jax.ShapeDtypeStruct((n, LANES), F32),
        ],
        compiler_params=pltpu.CompilerParams(
            dimension_semantics=("arbitrary", "arbitrary"), vmem_limit_bytes=VMEM_LIMIT),
        name="inproj",
    )(x2d, g, w_main, w_if, b_col)


def _mlstm_kernel(qk_ref, v_ref, o_ref, gc_ref, cw_ref, cb_ref, mhg_ref,
                  ya_ref, tail_ref, c_ref, n_ref, m_ref):
    L = CHUNK
    H = MLSTM_HEADS
    dqk = qk_ref.shape[-1]
    dh = dqk // (2 * H)

    @pl.when(pl.program_id(1) == 0)
    def _():
        tail_ref[...] = jnp.zeros_like(tail_ref)
        c_ref[...] = jnp.zeros_like(c_ref)
        n_ref[...] = jnp.zeros_like(n_ref)
        m_ref[...] = jnp.zeros_like(m_ref)

    cur = qk_ref[0].astype(F32)
    tail = tail_ref[...]
    row8 = lax.broadcasted_iota(I32, (CONV_HALO, dqk), 0)
    acc = cb_ref[...] + cw_ref[CONV_WIDTH - 1:CONV_WIDTH, :] * cur
    for k in range(1, CONV_WIDTH):
        rolled = pltpu.roll(cur, k, axis=0)
        head = jnp.where(row8 < k, pltpu.roll(tail, k, axis=0), rolled[0:CONV_HALO])
        shifted = jnp.concatenate([head, rolled[CONV_HALO:]], axis=0)
        acc = acc + cw_ref[CONV_WIDTH - 1 - k:CONV_WIDTH - k, :] * shifted
    act = acc * _sigmoid(acc)
    tail_ref[...] = cur[L - CONV_HALO:L]

    gc = gc_ref[0]
    gr = gc.T[0:8, :]
    row = lax.broadcasted_iota(I32, (L, L), 0)
    col = lax.broadcasted_iota(I32, (L, L), 1)
    causal = row >= col
    tril = causal.astype(BF16)
    triu = (row <= col).astype(BF16)
    b_c = _split3_dot(_log_sigmoid(gc), tril, "ta")
    b_r = _split3_dot(_log_sigmoid(gr), triu, "at")

    q_all = act[:, 0:H * dh]
    k_all = act[:, H * dh:2 * H * dh] * (dh ** -0.5)
    qb_all = q_all.astype(BF16)
    kb_all = k_all.astype(BF16)

    for h in range(H):
        q = q_all[:, h * dh:(h + 1) * dh]
        k = k_all[:, h * dh:(h + 1) * dh]
        qb = qb_all[:, h * dh:(h + 1) * dh]
        kb = kb_all[:, h * dh:(h + 1) * dh]
        v = v_ref[0, :, h * dh:(h + 1) * dh]
        bcol = b_c[:, H + h:H + h + 1]
        icol = gc[:, h:h + 1]
        brow = b_r[H + h:H + h + 1, :]
        irow = gr[h:h + 1, :]
        m_prev = m_ref[h:h + 1, 0:1]

        dmat = jnp.where(causal, bcol - brow + irow, -jnp.inf)
        inter = bcol + m_prev
        m_t = jnp.maximum(inter, jnp.max(dmat, axis=-1, keepdims=True))
        dexp = jnp.exp(dmat - m_t)
        inter_w = jnp.exp(inter - m_t)
        s = lax.dot_general(qb, kb, (((1,), (1,)), ((), ())), preferred_element_type=F32) * dexp
        c_old = c_ref[h]
        n_old = n_ref[h:h + 1, :]
        num = (jnp.dot(s.astype(BF16), v, preferred_element_type=F32)
               + inter_w * jnp.dot(qb, c_old.astype(BF16), preferred_element_type=F32))
        den = (jnp.sum(s, axis=-1, keepdims=True)
               + inter_w * jnp.sum(q * n_old, axis=-1, keepdims=True))
        hh = num * (1.0 / jnp.maximum(jnp.abs(den), jnp.exp(-m_t)))

        b_last = bcol[L - 1:L, :]
        wlog = b_last - bcol + icol
        m_new = jnp.maximum(b_last + m_prev, jnp.max(wlog, axis=0, keepdims=True))
        decay = jnp.exp(b_last + m_prev - m_new)
        kw = k * jnp.exp(wlog - m_new)
        c_ref[h] = decay * c_old + jnp.dot(kw.T.astype(BF16), v, preferred_element_type=F32)
        n_ref[h:h + 1, :] = decay * n_old + jnp.sum(kw, axis=0, keepdims=True)
        m_ref[h:h + 1, :] = jnp.broadcast_to(m_new, (1, LANES))

        y = _rms(hh, mhg_ref[:, h * dh:(h + 1) * dh])
        og = _sigmoid(o_ref[0, :, h * dh:(h + 1) * dh].astype(F32))
        ya_ref[0, :, h * dh:(h + 1) * dh] = (y * og).astype(BF16)


def _mlstm(proj3, gc3, conv_w, conv_b, mh_g, d):
    b, s, _ = proj3.shape
    nc = s // CHUNK
    dqk = 2 * d
    dh = d // MLSTM_HEADS
    return pl.pallas_call(
        _mlstm_kernel,
        grid=(b, nc),
        in_specs=[
            pl.BlockSpec((1, CHUNK, dqk), lambda bi, ci: (bi, ci, 0)),
            pl.BlockSpec((1, CHUNK, d), lambda bi, ci: (bi, ci, 2)),
            pl.BlockSpec((1, CHUNK, d), lambda bi, ci: (bi, ci, 3)),
            pl.BlockSpec((1, CHUNK, LANES), lambda bi, ci: (bi, ci, 0)),
            pl.BlockSpec((CONV_WIDTH, dqk), lambda bi, ci: (0, 0)),
            pl.BlockSpec((1, dqk), lambda bi, ci: (0, 0)),
            pl.BlockSpec((1, d), lambda bi, ci: (0, 0)),
        ],
        out_specs=pl.BlockSpec((1, CHUNK, d), lambda bi, ci: (bi, ci, 0)),
        out_shape=jax.ShapeDtypeStruct((b, s, d), BF16),
        scratch_shapes=[
            pltpu.VMEM((CONV_HALO, dqk), F32),
            pltpu.VMEM((MLSTM_HEADS, dh, dh), F32),
            pltpu.VMEM((8, dh), F32),
            pltpu.VMEM((8, LANES), F32),
        ],
        compiler_params=pltpu.CompilerParams(
            dimension_semantics=("arbitrary", "arbitrary"), vmem_limit_bytes=VMEM_LIMIT),
        name="mlstm",
    )(proj3, proj3, proj3, gc3, conv_w, conv_b, mh_g)


def _merge_kernel(u_ref, sv_ref, ga_ref, gb_ref, ya_ref, x_ref, ws_ref, bst_ref, sg_ref, bg_ref,
                  wout_ref, x1_ref, yb_ref):
    tm, d = x_ref.shape
    L = CHUNK
    G = SGU_GROUPS
    gd = d // G
    uu = _gelu_tanh(u_ref[...].astype(F32))
    vn = _rms(_gelu_tanh(sv_ref[...].astype(F32)), sg_ref[...]).astype(BF16)
    row = lax.broadcasted_iota(I32, (L, L), 0)
    col = lax.broadcasted_iota(I32, (L, L), 1)
    causal = row >= col
    for g in range(G):
        wg = jnp.where(causal, ws_ref[g], 0.0).astype(BF16)
        bias = bst_ref[:, g:g + 1]
        for ci in range(tm // L):
            mixed = jnp.dot(wg, vn[ci * L:(ci + 1) * L, g * gd:(g + 1) * gd],
                            preferred_element_type=F32)
            yb_ref[ci * L:(ci + 1) * L, g * gd:(g + 1) * gd] = (
                uu[ci * L:(ci + 1) * L, g * gd:(g + 1) * gd] * (mixed + bias))
    g_a = _sigmoid(ga_ref[...].astype(F32) + bg_ref[:, 0:d])
    g_b = _sigmoid(gb_ref[...].astype(F32) + bg_ref[:, d:2 * d])
    merged = (g_a * ya_ref[...].astype(F32) + g_b * yb_ref[...]).astype(BF16)
    x1_ref[...] = x_ref[...] + jnp.dot(merged, wout_ref[...], preferred_element_type=F32)


def _merge(proj, ya, x2d, w_s, b_st, sgu_g, b_gate, w_out, tm):
    n, d = x2d.shape
    full = lambda shape: pl.BlockSpec(shape, lambda i: (0,) * len(shape))
    return pl.pallas_call(
        _merge_kernel,
        grid=(n // tm,),
        in_specs=[
            pl.BlockSpec((tm, d), lambda i: (i, 4)),
            pl.BlockSpec((tm, d), lambda i: (i, 5)),
            pl.BlockSpec((tm, d), lambda i: (i, 6)),
            pl.BlockSpec((tm, d), lambda i: (i, 7)),
            pl.BlockSpec((tm, d), lambda i: (i, 0)),
            pl.BlockSpec((tm, d), lambda i: (i, 0)),
            full(w_s.shape), full(b_st.shape), full(sgu_g.shape), full(b_gate.shape), full(w_out.shape),
        ],
        out_specs=pl.BlockSpec((tm, d), lambda i: (i, 0)),
        out_shape=jax.ShapeDtypeStruct((n, d), F32),
        scratch_shapes=[pltpu.VMEM((tm, d), F32)],
        compiler_params=pltpu.CompilerParams(
            dimension_semantics=("arbitrary",), vmem_limit_bytes=VMEM_LIMIT),
        name="merge",
    )(proj, proj, proj, proj, ya, x2d, w_s, b_st, sgu_g, b_gate, w_out)


def _memkv_kernel(mem_ref, g_ref, w_ref, kv_ref):
    mn = _rms(mem_ref[0], g_ref[...]).astype(BF16)
    kv_ref[0] = jnp.dot(mn, w_ref[...], preferred_element_type=F32).astype(BF16)


def _memkv(mem, g, w_xkv):
    b, m, d = mem.shape
    dk = w_xkv.shape[1]
    return pl.pallas_call(
        _memkv_kernel,
        grid=(b,),
        in_specs=[
            pl.BlockSpec((1, m, d), lambda i: (i, 0, 0)),
            pl.BlockSpec((1, d), lambda i: (0, 0)),
            pl.BlockSpec((d, dk), lambda i: (0, 0)),
        ],
        out_specs=pl.BlockSpec((1, m, dk), lambda i: (i, 0, 0)),
        out_shape=jax.ShapeDtypeStruct((b, m, dk), BF16),
        compiler_params=pltpu.CompilerParams(
            dimension_semantics=("arbitrary",), vmem_limit_bytes=VMEM_LIMIT),
        name="memkv",
    )(mem, g, w_xkv)


def _xattn_kernel(x1_ref, kv_ref, gx_ref, wq_ref, wo_ref, gm_ref, wr_ref, br_ref,
                  x2_ref, xn2_ref, lg_ref, att_ref):
    tm, d = x1_ref.shape[1:]
    H = XATTN_HEADS
    dh = d // H
    x1 = x1_ref[0]
    xn = _rms(x1, gx_ref[...]).astype(BF16)
    q = (jnp.dot(xn, wq_ref[...], preferred_element_type=F32) * (dh ** -0.5)).astype(BF16)
    for h in range(H):
        kh = kv_ref[0, :, h * dh:(h + 1) * dh]
        vh = kv_ref[0, :, d + h * dh:d + (h + 1) * dh]
        s = lax.dot_general(q[:, h * dh:(h + 1) * dh], kh, (((1,), (1,)), ((), ())),
                            preferred_element_type=F32)
        e = jnp.exp(s - jnp.max(s, axis=-1, keepdims=True))
        p = e * (1.0 / jnp.sum(e, axis=-1, keepdims=True))
        att_ref[:, h * dh:(h + 1) * dh] = jnp.dot(p.astype(BF16), vh,
                                                 preferred_element_type=F32).astype(BF16)
    x2 = x1 + jnp.dot(att_ref[...], wo_ref[...], preferred_element_type=F32)
    x2_ref[0] = x2
    xn2 = _rms(x2, gm_ref[...])
    xn2_ref[0] = xn2
    lg_ref[0] = _split3_both(xn2, wr_ref[...]) + br_ref[...]


def _xattn(x1_3d, kv, gx, w_xq, w_xo, gm, w_r, b_r, tm):
    b, s, d = x1_3d.shape
    m = kv.shape[1]
    full = lambda shape: pl.BlockSpec(shape, lambda bi, i: (0,) * len(shape))
    return pl.pallas_call(
        _xattn_kernel,
        grid=(b, s // tm),
        in_specs=[
            pl.BlockSpec((1, tm, d), lambda bi, i: (bi, i, 0)),
            pl.BlockSpec((1, m, 2 * d), lambda bi, i: (bi, 0, 0)),
            full(gx.shape), full(w_xq.shape), full(w_xo.shape), full(gm.shape), full(w_r.shape),
            full(b_r.shape),
        ],
        out_specs=[
            pl.BlockSpec((1, tm, d), lambda bi, i: (bi, i, 0)),
            pl.BlockSpec((1, tm, d), lambda bi, i: (bi, i, 0)),
            pl.BlockSpec((1, tm, LANES), lambda bi, i: (bi, i, 0)),
        ],
        out_shape=[
            jax.ShapeDtypeStruct((b, s, d), F32),
            jax.ShapeDtypeStruct((b, s, d), F32),
            jax.ShapeDtypeStruct((b, s, LANES), F32),
        ],
        scratch_shapes=[pltpu.VMEM((tm, d), BF16)],
        compiler_params=pltpu.CompilerParams(
            dimension_semantics=("arbitrary", "arbitrary"), vmem_limit_bytes=VMEM_LIMIT),
        name="xattn",
    )(x1_3d, kv, gx, w_xq, w_xo, gm, w_r, b_r)


def _router_kernel(lg_ref, ids_ref, gate_ref, cnt_ref, carry_ref):
    tm = lg_ref.shape[0]
    G, E = MOE_GROUPS, N_EXPERTS

    @pl.when(pl.program_id(0) == 0)
    def _():
        carry_ref[...] = jnp.zeros_like(carry_ref)

    lg = lg_ref[...]
    lane = lax.broadcasted_iota(I32, (tm, LANES), 1)
    lane_f = lane.astype(F32)
    ninf = -jnp.inf

    def first_lane(mask):
        return jnp.min(jnp.where(mask, lane_f, float(LANES)), axis=-1, keepdims=True).astype(I32)

    glm = jnp.where(lane < G, lg, ninf)
    gmax = jnp.max(glm, axis=-1, keepdims=True)
    g_idx = first_lane(glm == gmax)
    p_g = 1.0 / jnp.sum(jnp.exp(glm - gmax), axis=-1, keepdims=True)

    lo = G + g_idx * EXPERTS_PER_GROUP
    em = jnp.where((lane >= lo) & (lane < lo + EXPERTS_PER_GROUP), lg, ninf)
    top1 = jnp.max(em, axis=-1, keepdims=True)
    i1 = first_lane(em == top1)
    em2 = jnp.where(lane == i1, ninf, em)
    top2 = jnp.max(em2, axis=-1, keepdims=True)
    i2 = first_lane(em2 == top2)
    e2 = jnp.exp(top2 - top1)
    inv = 1.0 / (1.0 + e2)
    gate1 = p_g * inv
    gate2 = p_g * (e2 * inv)

    hot1 = lane == i1
    hot2 = lane == i2
    onehot = (hot1 | hot2).astype(BF16)
    r = lax.broadcasted_iota(I32, (tm, tm), 0)
    c = lax.broadcasted_iota(I32, (tm, tm), 1)
    strict = (r > c).astype(BF16)
    before = jnp.dot(strict, onehot, preferred_element_type=F32) + carry_ref[...]
    rank1 = jnp.sum(jnp.where(hot1, before, 0.0), axis=-1, keepdims=True)
    rank2 = jnp.sum(jnp.where(hot2, before, 0.0), axis=-1, keepdims=True)
    carry_ref[...] = carry_ref[...] + jnp.sum(onehot.astype(F32), axis=0, keepdims=True)

    ids = jnp.where(lane == 0, i1, 0)
    ids = jnp.where(lane == 1, i2, ids)
    ids = jnp.where(lane == 2, rank1.astype(I32), ids)
    ids = jnp.where(lane == 3, rank2.astype(I32), ids)
    ids_ref[...] = ids
    gate_ref[...] = jnp.where(lane == 0, gate1, jnp.where(lane == 1, gate2, 0.0))
    cnt_ref[...] = carry_ref[...]


def _router(logits, tm):
    n = logits.shape[0]
    return pl.pallas_call(
        _router_kernel,
        grid=(n // tm,),
        in_specs=[pl.BlockSpec((tm, LANES), lambda i: (i, 0))],
        out_specs=[
            pl.BlockSpec((tm, LANES), lambda i: (i, 0)),
            pl.BlockSpec((tm, LANES), lambda i: (i, 0)),
            pl.BlockSpec((1, LANES), lambda i: (0, 0)),
        ],
        out_shape=[
            jax.ShapeDtypeStruct((n, LANES), I32),
            jax.ShapeDtypeStruct((n, LANES), F32),
            jax.ShapeDtypeStruct((1, LANES), F32),
        ],
        scratch_shapes=[pltpu.VMEM((1, LANES), F32)],
        compiler_params=pltpu.CompilerParams(dimension_semantics=("arbitrary",)),
        name="router",
    )(logits)


def _plan_kernel(ids_ref, cnt_ref, dest_ref):
    tm = ids_ref.shape[0]
    ids = ids_ref[...]
    cnt = cnt_ref[...].astype(I32)
    padded = ((cnt + (ROW_BLOCK - 1)) >> ROW_BLOCK_BITS) << ROW_BLOCK_BITS
    k = lax.broadcasted_iota(I32, (LANES, LANES), 0)
    l = lax.broadcasted_iota(I32, (LANES, LANES), 1)
    before = (k < l).astype(BF16)
    pstart = _split3_dot(jnp.broadcast_to(padded.astype(F32), (8, LANES)), before, "at")[0:1]
    lane = lax.broadcasted_iota(I32, (tm, LANES), 1)

    def dest(col):
        start = jnp.sum(jnp.where(lane == ids[:, col:col + 1], pstart, 0.0), axis=-1, keepdims=True)
        return start + ids[:, col + 2:col + 3].astype(F32)

    d = jnp.where(lane == 0, dest(0), jnp.where(lane == 1, dest(1), 0.0))
    dest_ref[...] = d.T[0:8, :].astype(I32)


def _plan(ids, counts, tm):
    n = ids.shape[0]
    return pl.pallas_call(
        _plan_kernel,
        grid=(n // tm,),
        in_specs=[pl.BlockSpec((tm, LANES), lambda i: (i, 0)),
                  pl.BlockSpec((1, LANES), lambda i: (0, 0))],
        out_specs=pl.BlockSpec((8, tm), lambda i: (0, i)),
        out_shape=jax.ShapeDtypeStruct((8, n), I32),
        compiler_params=pltpu.CompilerParams(dimension_semantics=("arbitrary",)),
        name="plan",
    )(ids, counts)


def _dispatch_kernel(padstart_ref, padlen_ref, nvalid_ref, dest_hbm, x_ref, xs_hbm,
                     dsm, xcopy, zbuf, dsem, ssem, zsem):
    i = pl.program_id(0)
    nt = pl.num_programs(0)
    tm = x_ref.shape[0]
    slot = i % 2
    nb = xs_hbm.shape[0] // ROW_BLOCK

    def dest_copy(tile, k):
        buf = tile % 3
        return pltpu.make_async_copy(dest_hbm.at[k, pl.ds(tile * tm, tm)], dsm.at[buf, k],
                                     dsem.at[buf, k])

    def row_copy(r, k):
        return pltpu.make_async_copy(xcopy.at[slot, pl.ds(r, 1), :],
                                     xs_hbm.at[pl.ds(dsm[i % 3, k, r], 1), :], ssem.at[slot])

    def wait_rows(buf):
        for _ in range(MOE_TOPK):
            pltpu.make_async_copy(xcopy.at[buf], xs_hbm.at[pl.ds(0, tm), :], ssem.at[buf]).wait()

    def zero_copies(e):
        start = padstart_ref[e]
        n_pad = padlen_ref[e]
        head = jnp.minimum((-start) & 7, n_pad)
        for t in range(7):
            yield t < head, pltpu.make_async_copy(
                zbuf.at[pl.ds(0, 1), :], xs_hbm.at[pl.ds(start + t, 1), :], zsem)
        rest = n_pad - head
        for bit in range(3, ROW_BLOCK_BITS):
            size = 1 << bit
            off = pl.multiple_of(start + head + (rest & (size - 1)), 8)
            yield ((rest >> bit) & 1) == 1, pltpu.make_async_copy(
                zbuf.at[pl.ds(0, size), :], xs_hbm.at[pl.ds(off, size), :], zsem)

    def unused_block_copies():
        for blk in range(nb - N_EXPERTS, nb):
            yield blk >= nvalid_ref[0], pltpu.make_async_copy(
                zbuf, xs_hbm.at[pl.ds(blk * ROW_BLOCK, ROW_BLOCK), :], zsem)

    @pl.when(i == 0)
    def _():
        zbuf[...] = jnp.zeros_like(zbuf)
        for flag, cp in unused_block_copies():
            pl.when(flag)(cp.start)
        for e in range(N_EXPERTS):
            for flag, cp in zero_copies(e):
                pl.when(flag)(cp.start)
        for k in range(MOE_TOPK):
            dest_copy(0, k).start()

            @pl.when(nt > 1)
            def _():
                dest_copy(1, k).start()

    for k in range(MOE_TOPK):
        dest_copy(i, k).wait()

        @pl.when(i + 2 < nt)
        def _():
            dest_copy(i + 2, k).start()

    xcopy[slot] = x_ref[...]
    for r in range(tm):
        for k in range(MOE_TOPK):
            row_copy(r, k).start()

    @pl.when(i >= 1)
    def _():
        wait_rows(1 - slot)

    @pl.when(i == nt - 1)
    def _():
        wait_rows(slot)
        for flag, cp in unused_block_copies():
            pl.when(flag)(cp.wait)
        for e in range(N_EXPERTS):
            for flag, cp in zero_copies(e):
                pl.when(flag)(cp.wait)


def _dispatch(pad_start, pad_len, nvalid, dest_t, xn2, p_rows):
    n, d = xn2.shape
    tm = MOE_TILE
    grid_spec = pltpu.PrefetchScalarGridSpec(
        num_scalar_prefetch=3,
        grid=(n // tm,),
        in_specs=[
            pl.BlockSpec(memory_space=pl.ANY),
            pl.BlockSpec((tm, d), lambda i, ps, pn, nv: (i, 0)),
        ],
        out_specs=pl.BlockSpec(memory_space=pl.ANY),
        scratch_shapes=[
            pltpu.SMEM((3, MOE_TOPK, tm), I32),
            pltpu.VMEM((2, tm, d), F32),
            pltpu.VMEM((ROW_BLOCK, d), F32),
            pltpu.SemaphoreType.DMA((3, MOE_TOPK)),
            pltpu.SemaphoreType.DMA((2,)),
            pltpu.SemaphoreType.DMA,
        ],
    )
    return pl.pallas_call(
        _dispatch_kernel,
        grid_spec=grid_spec,
        out_shape=jax.ShapeDtypeStruct((p_rows, d), F32),
        compiler_params=pltpu.CompilerParams(
            dimension_semantics=("arbitrary",), vmem_limit_bytes=VMEM_LIMIT),
        name="dispatch",
    )(pad_start, pad_len, nvalid, dest_t, xn2)


def _ffn_kernel(bexp_ref, nvalid_ref, xs_ref, w1_ref, w3_ref, w2_ref, o_ref, w1b, w3b, w2b):
    i = pl.program_id(0)
    nvalid = nvalid_ref[0]

    @pl.when(i < nvalid)
    def _():
        changed = jnp.logical_or(i == 0, bexp_ref[i] != bexp_ref[jnp.maximum(i - 1, 0)])

        @pl.when(changed)
        def _():
            w1b[...] = w1_ref[0].astype(BF16)
            w3b[...] = w3_ref[0].astype(BF16)
            w2b[...] = w2_ref[0].astype(BF16)

        xb = xs_ref[...].astype(BF16)
        a = jnp.dot(xb, w1b[...], preferred_element_type=F32)
        g = jnp.dot(xb, w3b[...], preferred_element_type=F32)
        hb = (a * _sigmoid(a) * g).astype(BF16)
        o_ref[...] = jnp.dot(hb, w2b[...], preferred_element_type=F32)

    @pl.when(i >= nvalid)
    def _():
        o_ref[...] = jnp.zeros_like(o_ref)


def _ffn(block_expert, nvalid, xs, w1, w3, w2):
    p_rows, d = xs.shape
    tb = ROW_BLOCK
    _, _, f = w1.shape
    grid_spec = pltpu.PrefetchScalarGridSpec(
        num_scalar_prefetch=2,
        grid=(p_rows // tb,),
        in_specs=[
            pl.BlockSpec((tb, d), lambda i, be, nv: (jnp.minimum(i, nv[0] - 1), 0)),
            pl.BlockSpec((1, d, f), lambda i, be, nv: (be[i], 0, 0)),
            pl.BlockSpec((1, d, f), lambda i, be, nv: (be[i], 0, 0)),
            pl.BlockSpec((1, f, d), lambda i, be, nv: (be[i], 0, 0)),
        ],
        out_specs=pl.BlockSpec((tb, d), lambda i, be, nv: (i, 0)),
        scratch_shapes=[
            pltpu.VMEM((d, f), BF16),
            pltpu.VMEM((d, f), BF16),
            pltpu.VMEM((f, d), BF16),
        ],
    )
    return pl.pallas_call(
        _ffn_kernel,
        grid_spec=grid_spec,
        out_shape=jax.ShapeDtypeStruct((p_rows, d), F32),
        compiler_params=pltpu.CompilerParams(
            dimension_semantics=("arbitrary",), vmem_limit_bytes=VMEM_LIMIT),
        name="ffn",
    )(block_expert, nvalid, xs, w1, w3, w2)


def _combine_kernel(dest_hbm, y_hbm, x2_ref, gate_ref, g_ref, out_ref, dsm, ybuf, dsem, gsem):
    j = pl.program_id(0)
    nt = pl.num_programs(0) - 1
    tm = x2_ref.shape[0]

    def dest_copy(tile, k):
        buf = tile % 3
        return pltpu.make_async_copy(dest_hbm.at[k, pl.ds(tile * tm, tm)], dsm.at[buf, k],
                                     dsem.at[buf, k])

    def row_copy(r, k):
        return pltpu.make_async_copy(y_hbm.at[pl.ds(dsm[j % 3, k, r], 1), :],
                                     ybuf.at[j % 2, k, pl.ds(r, 1), :], gsem.at[j % 2])

    @pl.when(j == 0)
    def _():
        for k in range(MOE_TOPK):
            dest_copy(0, k).start()

    @pl.when(j < nt)
    def _():
        for k in range(MOE_TOPK):
            dest_copy(j, k).wait()

            @pl.when(j + 1 < nt)
            def _():
                dest_copy(j + 1, k).start()

        for r in range(tm):
            for k in range(MOE_TOPK):
                row_copy(r, k).start()

    @pl.when(j >= 1)
    def _():
        buf = (j - 1) % 2
        for k in range(MOE_TOPK):
            pltpu.make_async_copy(y_hbm.at[pl.ds(0, tm), :], ybuf.at[buf, k], gsem.at[buf]).wait()
        gate = gate_ref[...]
        x = x2_ref[...] + gate[:, 0:1] * ybuf[buf, 0] + gate[:, 1:2] * ybuf[buf, 1]
        out_ref[...] = _rms(x, g_ref[...])


def _combine(dest_t, y, x2, gates, g):
    n, d = x2.shape
    tm = MOE_TILE
    nt = n // tm
    prev = lambda j: (jnp.maximum(j - 1, 0), 0)
    return pl.pallas_call(
        _combine_kernel,
        grid=(nt + 1,),
        in_specs=[
            pl.BlockSpec(memory_space=pl.ANY),
            pl.BlockSpec(memory_space=pl.ANY),
            pl.BlockSpec((tm, d), prev),
            pl.BlockSpec((tm, LANES), prev),
            pl.BlockSpec((1, d), lambda j: (0, 0)),
        ],
        out_specs=pl.BlockSpec((tm, d), prev),
        out_shape=jax.ShapeDtypeStruct((n, d), F32),
        scratch_shapes=[
            pltpu.SMEM((3, MOE_TOPK, tm), I32),
            pltpu.VMEM((2, MOE_TOPK, tm, d), F32),
            pltpu.SemaphoreType.DMA((3, MOE_TOPK)),
            pltpu.SemaphoreType.DMA((2,)),
        ],
        compiler_params=pltpu.CompilerParams(
            dimension_semantics=("arbitrary",), vmem_limit_bytes=VMEM_LIMIT),
        name="combine",
    )(dest_t, y, x2, gates, g)


def kernel(x, mem, norm_mix_g, w_in, b_gate, b_if, conv_w, conv_b, mh_norm_g, sgu_norm_g, w_s, b_s, w_out, norm_x_g, norm_mem_g, w_xq, w_xkv, w_xo, norm_moe_g, w_rg, b_rg, w_re, b_re, w1, w3, w2, norm_f_g):
    assert w_in.shape[0] == 1, "the combine stage fuses the final norm: one layer only"
    (norm_mix_g, w_in, b_gate, b_if, conv_w, conv_b, mh_norm_g, sgu_norm_g, w_s, b_s, w_out, norm_x_g,
     norm_mem_g, w_xq, w_xkv, w_xo, norm_moe_g, w_rg, b_rg, w_re, b_re, w1, w3, w2) = (
        p[0] for p in (norm_mix_g, w_in, b_gate, b_if, conv_w, conv_b, mh_norm_g, sgu_norm_g, w_s, b_s,
                       w_out, norm_x_g, norm_mem_g, w_xq, w_xkv, w_xo, norm_moe_g, w_rg, b_rg, w_re,
                       b_re, w1, w3, w2))
    b, s, d = x.shape
    n = b * s
    H = MLSTM_HEADS
    x2d = x.reshape(n, d)
    row = lambda v: v.reshape(1, -1)

    c_if = 4 * d
    w_main = jnp.concatenate([w_in[:, :c_if], w_in[:, c_if + 2 * H:]], axis=1).astype(BF16)
    w_if = w_in[:, c_if:c_if + 2 * H]
    w_if_col = jnp.pad(w_if, ((0, 0), (0, LANES - 2 * H))).astype(BF16)
    b_col = jnp.pad(b_if, (0, LANES - 2 * H)).reshape(1, LANES)

    proj, gc = _inproj(x2d, row(norm_mix_g), w_main, w_if_col, b_col, tm=min(1024, n), tn=2048)
    ya = _mlstm(proj.reshape(b, s, -1), gc.reshape(b, s, LANES), conv_w, row(conv_b),
                row(mh_norm_g), d)
    x1 = _merge(proj, ya.reshape(n, d), x2d, w_s, b_s.T, row(sgu_norm_g), row(b_gate),
                w_out.astype(BF16), tm=min(512, n))

    kv = _memkv(mem, row(norm_mem_g), w_xkv.astype(BF16))
    n_route = MOE_GROUPS + N_EXPERTS
    w_r = jnp.pad(jnp.concatenate([w_rg, w_re], axis=1), ((0, 0), (0, LANES - n_route)))
    b_r = jnp.pad(jnp.concatenate([b_rg, b_re]), (0, LANES - n_route)).reshape(1, LANES)
    x2, xn2, logits = _xattn(x1.reshape(b, s, d), kv, row(norm_x_g), w_xq.astype(BF16),
                             w_xo.astype(BF16), row(norm_moe_g), w_r, b_r, tm=min(512, s))
    x2 = x2.reshape(n, d)
    xn2 = xn2.reshape(n, d)

    ids, gates, counts = _router(logits.reshape(n, LANES), tm=min(512, n))
    dest_t = _plan(ids, counts, tm=min(512, n))

    tb = ROW_BLOCK
    nb = n * MOE_TOPK // tb + N_EXPERTS
    cnt = counts[0, MOE_GROUPS:n_route].astype(I32)
    padded = (cnt + tb - 1) // tb * tb
    pends = jnp.cumsum(padded)
    block_start = jnp.arange(nb, dtype=I32) * tb
    block_expert = jnp.minimum(jnp.sum(pends[None, :] <= block_start[:, None], axis=1),
                               N_EXPERTS - 1).astype(I32)
    nvalid = (pends[-1] // tb).astype(I32).reshape(1)
    pad_start = (pends - padded + cnt).astype(I32)
    pad_len = (padded - cnt).astype(I32)

    xs = _dispatch(pad_start, pad_len, nvalid, dest_t, xn2, nb * tb)
    y = _ffn(block_expert, nvalid, xs, w1, w3, w2)
    return _combine(dest_t, y, x2, gates, row(norm_f_g)).reshape(b, s, d)
```

```python
import functools

import jax
import jax.numpy as jnp
from jax import lax
from jax.experimental import pallas as pl
from jax.experimental.pallas import tpu as pltpu

F32 = jnp.float32
BF16 = jnp.bfloat16
I32 = jnp.int32

EPS = 1e-6
LANES = 128
CHUNK = 128
MLSTM_HEADS = 4
SGU_GROUPS = 8
XATTN_HEADS = 4
MOE_GROUPS = 4
EXPERTS_PER_GROUP = 8
N_EXPERTS = MOE_GROUPS * EXPERTS_PER_GROUP
MOE_TOPK = 2
CONV_WIDTH = 4
CONV_HALO = 8
ROW_BLOCK = 256
ROW_BLOCK_BITS = ROW_BLOCK.bit_length() - 1
MOE_TILE = 256
VMEM_LIMIT = 56 * 1024 * 1024


def _sigmoid(x):
    return 0.5 * jnp.tanh(0.5 * x) + 0.5


def _rms(x, g):
    return x * lax.rsqrt(jnp.mean(x * x, axis=-1, keepdims=True) + EPS) * g


def _log_sigmoid(x):
    return jnp.minimum(x, 0.0) - jnp.log(1.0 + jnp.exp(-jnp.abs(x)))


def _gelu_tanh(x):
    c = 0.7978845608028654
    return 0.5 * x * (1.0 + jnp.tanh(c * (x + 0.044715 * (x * x * x))))


def _split3_dot(a, tri, dims):
    a_hi = a.astype(BF16)
    r1 = a - a_hi.astype(F32)
    a_mid = r1.astype(BF16)
    a_lo = (r1 - a_mid.astype(F32)).astype(BF16)
    if dims == "at":
        f = lambda p: jnp.dot(p, tri, preferred_element_type=F32)
    else:
        f = lambda p: jnp.dot(tri, p, preferred_element_type=F32)
    return f(a_hi) + f(a_mid) + f(a_lo)


def _split3_both(a, b):
    a_hi = a.astype(BF16)
    a_lo = (a - a_hi.astype(F32)).astype(BF16)
    b_hi = b.astype(BF16)
    b_lo = (b - b_hi.astype(F32)).astype(BF16)
    f = lambda p, r: jnp.dot(p, r, preferred_element_type=F32)
    return f(a_hi, b_hi) + (f(a_hi, b_lo) + f(a_lo, b_hi))


def _proj(x_ref, g_ref, w_ref):
    xn = _rms(x_ref[...], g_ref[...]).astype(BF16)
    return xn, jnp.dot(xn, w_ref[...], preferred_element_type=F32)


def _proj_qk_kernel(x_ref, g_ref, w_ref, cw_ref, cb_ref, qk_ref, tail_ref, *, tiles_per_seq):
    tm, dqk = qk_ref.shape
    groups = tm // CONV_HALO

    @pl.when(pl.program_id(0) % tiles_per_seq == 0)
    def _():
        tail_ref[...] = jnp.zeros_like(tail_ref)

    _, cur = _proj(x_ref, g_ref, w_ref)
    cur3 = cur.reshape(groups, CONV_HALO, dqk)
    tail3 = tail_ref[...].reshape(1, CONV_HALO, dqk)
    sub = lax.broadcasted_iota(I32, (1, CONV_HALO, dqk), 1)
    acc = cb_ref[...] + cw_ref[CONV_WIDTH - 1:CONV_WIDTH, :] * cur3
    for k in range(1, CONV_WIDTH):
        r_cur = pltpu.roll(cur3, k, axis=1)
        r_prev = jnp.concatenate([pltpu.roll(tail3, k, axis=1), r_cur[:groups - 1]], axis=0)
        acc = acc + cw_ref[CONV_WIDTH - 1 - k:CONV_WIDTH - k, :] * jnp.where(sub < k, r_prev, r_cur)
    tail_ref[...] = cur[tm - CONV_HALO:tm]
    qk_ref[...] = acc.reshape(tm, dqk).astype(BF16)


def _proj_vo_kernel(x_ref, g_ref, w_ref, vo_ref):
    _, p = _proj(x_ref, g_ref, w_ref)
    half = p.shape[1] // 2
    vo_ref[:, 0:half] = p[:, 0:half].astype(BF16)
    vo_ref[:, half:] = _sigmoid(p[:, half:]).astype(BF16)


def _proj_sgu_kernel(x_ref, g_ref, w_ref, sg_ref, usv_ref):
    _, p = _proj(x_ref, g_ref, w_ref)
    half = p.shape[1] // 2
    usv_ref[:, 0:half] = _gelu_tanh(p[:, 0:half]).astype(BF16)
    usv_ref[:, half:] = _rms(_gelu_tanh(p[:, half:]), sg_ref[...]).astype(BF16)


def _proj_gate_kernel(x_ref, g_ref, w_ref, bg_ref, wif_ref, bif_ref, gab_ref, gc_ref):
    xn, p = _proj(x_ref, g_ref, w_ref)
    gab_ref[...] = _sigmoid(p + bg_ref[...]).astype(BF16)
    gc_ref[...] = jnp.dot(xn, wif_ref[...], preferred_element_type=F32) + bif_ref[...]


def _proj_call(body, name, x2d, g, w, extra, out_shapes, scratch, tm):
    n, d = x2d.shape
    full = lambda a: pl.BlockSpec(a.shape, lambda i: (0,) * a.ndim)
    return pl.pallas_call(
        body,
        grid=(n // tm,),
        in_specs=[pl.BlockSpec((tm, d), lambda i: (i, 0)), full(g), full(w)] + [full(a) for a in extra],
        out_specs=[pl.BlockSpec((tm, o.shape[1]), lambda i: (i, 0)) for o in out_shapes],
        out_shape=out_shapes,
        scratch_shapes=scratch,
        compiler_params=pltpu.CompilerParams(
            dimension_semantics=("arbitrary",), vmem_limit_bytes=VMEM_LIMIT),
        name=name,
    )(x2d, g, w, *extra)


def _mlstm_kernel(qk_ref, v_ref, og_ref, gc_ref, mhg_ref, ya_ref, c_ref, n_ref, m_ref):
    L = CHUNK
    H = MLSTM_HEADS
    n_seq, _, dqk = qk_ref.shape
    dh = dqk // (2 * H)

    @pl.when(pl.program_id(0) == 0)
    def _():
        c_ref[...] = jnp.zeros_like(c_ref)
        n_ref[...] = jnp.zeros_like(n_ref)
        m_ref[...] = jnp.zeros_like(m_ref)

    row = lax.broadcasted_iota(I32, (L, L), 0)
    col = lax.broadcasted_iota(I32, (L, L), 1)
    causal = row >= col
    tril = causal.astype(BF16)
    triu = (row <= col).astype(BF16)
    results = []
    for bi in range(n_seq):
        results += _mlstm_chunk(bi, qk_ref, v_ref, og_ref, gc_ref, mhg_ref, c_ref, n_ref, m_ref,
                                causal, tril, triu, dh)
    for si, c_new, n_new, m_new, y in results:
        bi, h = divmod(si, H)
        c_ref[si] = c_new
        n_ref[si:si + 1, :] = n_new
        m_ref[si:si + 1, :] = jnp.broadcast_to(m_new, (1, LANES))
        ya_ref[bi, :, h * dh:(h + 1) * dh] = y


def _mlstm_chunk(bi, qk_ref, v_ref, og_ref, gc_ref, mhg_ref, c_ref, n_ref, m_ref,
                 causal, tril, triu, dh):
    L = CHUNK
    H = MLSTM_HEADS
    gc = gc_ref[bi]
    gr = gc.T[0:8, :]
    b_c = _split3_dot(_log_sigmoid(gc), tril, "ta")
    b_r = _split3_dot(_log_sigmoid(gr), triu, "at")

    half = 0.5 * qk_ref[bi].astype(F32)
    act = half + half * jnp.tanh(half)
    q_all = act[:, 0:H * dh]
    k_all = act[:, H * dh:2 * H * dh] * (dh ** -0.5)
    qb_all = q_all.astype(BF16)
    kb_all = k_all.astype(BF16)

    results = []
    for h in range(H):
        si = bi * H + h
        q = q_all[:, h * dh:(h + 1) * dh]
        k = k_all[:, h * dh:(h + 1) * dh]
        qb = qb_all[:, h * dh:(h + 1) * dh]
        kb = kb_all[:, h * dh:(h + 1) * dh]
        v = v_ref[bi, :, h * dh:(h + 1) * dh]
        bcol = b_c[:, H + h:H + h + 1]
        icol = gc[:, h:h + 1]
        brow = b_r[H + h:H + h + 1, :]
        irow = gr[h:h + 1, :]
        m_prev = m_ref[si:si + 1, 0:1]

        dmat = jnp.where(causal, bcol - brow + irow, -jnp.inf)
        inter = bcol + m_prev
        m_t = jnp.maximum(inter, jnp.max(dmat, axis=-1, keepdims=True))
        dexp = jnp.exp(dmat - m_t)
        inter_w = jnp.exp(inter - m_t)
        s = lax.dot_general(qb, kb, (((1,), (1,)), ((), ())), preferred_element_type=F32) * dexp
        c_old = c_ref[si]
        n_old = n_ref[si:si + 1, :]
        num = (jnp.dot(s.astype(BF16), v, preferred_element_type=F32)
               + inter_w * jnp.dot(qb, c_old.astype(BF16), preferred_element_type=F32))
        den = (jnp.sum(s, axis=-1, keepdims=True)
               + inter_w * jnp.sum(q * n_old, axis=-1, keepdims=True))
        hh = num * (1.0 / jnp.maximum(jnp.abs(den), jnp.exp(-m_t)))

        b_last = bcol[L - 1:L, :]
        wlog = b_last - bcol + icol
        m_new = jnp.maximum(b_last + m_prev, jnp.max(wlog, axis=0, keepdims=True))
        decay = jnp.exp(b_last + m_prev - m_new)
        kw = k * jnp.exp(wlog - m_new)
        c_new = decay * c_old + jnp.dot(kw.T.astype(BF16), v, preferred_element_type=F32)
        n_new = decay * n_old + jnp.sum(kw, axis=0, keepdims=True)
        y = _rms(hh, mhg_ref[:, h * dh:(h + 1) * dh])
        og = og_ref[bi, :, h * dh:(h + 1) * dh].astype(F32)
        results.append((si, c_new, n_new, m_new, (y * og).astype(BF16)))
    return results


def _mlstm(qk3, vo3, gc3, mh_g):
    b, s, dqk = qk3.shape
    nc = s // CHUNK
    d = dqk // 2
    dh = d // MLSTM_HEADS
    n_state = b * MLSTM_HEADS
    return pl.pallas_call(
        _mlstm_kernel,
        grid=(nc,),
        in_specs=[
            pl.BlockSpec((b, CHUNK, dqk), lambda ci: (0, ci, 0)),
            pl.BlockSpec((b, CHUNK, d), lambda ci: (0, ci, 0)),
            pl.BlockSpec((b, CHUNK, d), lambda ci: (0, ci, 1)),
            pl.BlockSpec((b, CHUNK, LANES), lambda ci: (0, ci, 0)),
            pl.BlockSpec((1, d), lambda ci: (0, 0)),
        ],
        out_specs=pl.BlockSpec((b, CHUNK, d), lambda ci: (0, ci, 0)),
        out_shape=jax.ShapeDtypeStruct((b, s, d), BF16),
        scratch_shapes=[
            pltpu.VMEM((n_state, dh, dh), F32),
            pltpu.VMEM((n_state, dh), F32),
            pltpu.VMEM((n_state, LANES), F32),
        ],
        compiler_params=pltpu.CompilerParams(
            dimension_semantics=("arbitrary",), vmem_limit_bytes=VMEM_LIMIT),
        name="mlstm",
    )(qk3, vo3, vo3, gc3, mh_g)


def _merge_kernel(u_ref, vn_ref, ga_ref, gb_ref, ya_ref, x_ref, ws_ref, bst_ref,
                  wout_ref, x1_ref, yb_ref):
    tm, d = x_ref.shape
    L = CHUNK
    G = SGU_GROUPS
    gd = d // G
    uu = u_ref[...].astype(F32)
    vn = vn_ref[...]
    row = lax.broadcasted_iota(I32, (L, L), 0)
    col = lax.broadcasted_iota(I32, (L, L), 1)
    causal = row >= col
    for g in range(G):
        wg = jnp.where(causal, ws_ref[g], 0.0).astype(BF16)
        bias = bst_ref[:, g:g + 1]
        for ci in range(tm // L):
            mixed = jnp.dot(wg, vn[ci * L:(ci + 1) * L, g * gd:(g + 1) * gd],
                            preferred_element_type=F32)
            yb_ref[ci * L:(ci + 1) * L, g * gd:(g + 1) * gd] = (
                uu[ci * L:(ci + 1) * L, g * gd:(g + 1) * gd] * (mixed + bias))
    merged = (ga_ref[...].astype(F32) * ya_ref[...].astype(F32)
              + gb_ref[...].astype(F32) * yb_ref[...]).astype(BF16)
    x1_ref[...] = x_ref[...] + jnp.dot(merged, wout_ref[...], preferred_element_type=F32)


def _merge(usv, gab, ya, x2d, w_s, b_st, w_out, tm):
    n, d = x2d.shape
    full = lambda shape: pl.BlockSpec(shape, lambda i: (0,) * len(shape))
    return pl.pallas_call(
        _merge_kernel,
        grid=(n // tm,),
        in_specs=[
            pl.BlockSpec((tm, d), lambda i: (i, 0)),
            pl.BlockSpec((tm, d), lambda i: (i, 1)),
            pl.BlockSpec((tm, d), lambda i: (i, 0)),
            pl.BlockSpec((tm, d), lambda i: (i, 1)),
            pl.BlockSpec((tm, d), lambda i: (i, 0)),
            pl.BlockSpec((tm, d), lambda i: (i, 0)),
            full(w_s.shape), full(b_st.shape), full(w_out.shape),
        ],
        out_specs=pl.BlockSpec((tm, d), lambda i: (i, 0)),
        out_shape=jax.ShapeDtypeStruct((n, d), F32),
        scratch_shapes=[pltpu.VMEM((tm, d), F32)],
        compiler_params=pltpu.CompilerParams(
            dimension_semantics=("arbitrary",), vmem_limit_bytes=VMEM_LIMIT),
        name="merge",
    )(usv, usv, gab, gab, ya, x2d, w_s, b_st, w_out)


def _memkv_kernel(mem_ref, g_ref, w_ref, kv_ref):
    mn = _rms(mem_ref[0], g_ref[...]).astype(BF16)
    kv_ref[0] = jnp.dot(mn, w_ref[...], preferred_element_type=F32).astype(BF16)


def _memkv(mem, g, w_xkv):
    b, m, d = mem.shape
    dk = w_xkv.shape[1]
    return pl.pallas_call(
        _memkv_kernel,
        grid=(b,),
        in_specs=[
            pl.BlockSpec((1, m, d), lambda i: (i, 0, 0)),
            pl.BlockSpec((1, d), lambda i: (0, 0)),
            pl.BlockSpec((d, dk), lambda i: (0, 0)),
        ],
        out_specs=pl.BlockSpec((1, m, dk), lambda i: (i, 0, 0)),
        out_shape=jax.ShapeDtypeStruct((b, m, dk), BF16),
        compiler_params=pltpu.CompilerParams(
            dimension_semantics=("arbitrary",), vmem_limit_bytes=VMEM_LIMIT),
        name="memkv",
    )(mem, g, w_xkv)


def _xattn_kernel(x1_ref, kv_ref, gx_ref, wq_ref, wo_ref, gm_ref, wr_ref, br_ref,
                  x2_ref, xn2_ref, lg_ref, att_ref):
    tm, d = x1_ref.shape[1:]
    H = XATTN_HEADS
    dh = d // H
    x1 = x1_ref[0]
    xn = _rms(x1, gx_ref[...]).astype(BF16)
    q = (jnp.dot(xn, wq_ref[...], preferred_element_type=F32) * (dh ** -0.5)).astype(BF16)
    for h in range(H):
        kh = kv_ref[0, :, h * dh:(h + 1) * dh]
        vh = kv_ref[0, :, d + h * dh:d + (h + 1) * dh]
        s = lax.dot_general(q[:, h * dh:(h + 1) * dh], kh, (((1,), (1,)), ((), ())),
                            preferred_element_type=F32)
        e = jnp.exp(s - jnp.max(s, axis=-1, keepdims=True))
        p = e * (1.0 / jnp.sum(e, axis=-1, keepdims=True))
        att_ref[:, h * dh:(h + 1) * dh] = jnp.dot(p.astype(BF16), vh,
                                                 preferred_element_type=F32).astype(BF16)
    x2 = x1 + jnp.dot(att_ref[...], wo_ref[...], preferred_element_type=F32)
    x2_ref[0] = x2
    xn2 = _rms(x2, gm_ref[...])
    xn2_ref[0] = xn2
    lg_ref[0] = _split3_both(xn2, wr_ref[...]) + br_ref[...]


def _xattn(x1_3d, kv, gx, w_xq, w_xo, gm, w_r, b_r, tm):
    b, s, d = x1_3d.shape
    m = kv.shape[1]
    full = lambda shape: pl.BlockSpec(shape, lambda bi, i: (0,) * len(shape))
    return pl.pallas_call(
        _xattn_kernel,
        grid=(b, s // tm),
        in_specs=[
            pl.BlockSpec((1, tm, d), lambda bi, i: (bi, i, 0)),
            pl.BlockSpec((1, m, 2 * d), lambda bi, i: (bi, 0, 0)),
            full(gx.shape), full(w_xq.shape), full(w_xo.shape), full(gm.shape), full(w_r.shape),
            full(b_r.shape),
        ],
        out_specs=[
            pl.BlockSpec((1, tm, d), lambda bi, i: (bi, i, 0)),
            pl.BlockSpec((1, tm, d), lambda bi, i: (bi, i, 0)),
            pl.BlockSpec((1, tm, LANES), lambda bi, i: (bi, i, 0)),
        ],
        out_shape=[
            jax.ShapeDtypeStruct((b, s, d), F32),
            jax.ShapeDtypeStruct((b, s, d), F32),
            jax.ShapeDtypeStruct((b, s, LANES), F32),
        ],
        scratch_shapes=[pltpu.VMEM((tm, d), BF16)],
        compiler_params=pltpu.CompilerParams(
            dimension_semantics=("arbitrary", "arbitrary"), vmem_limit_bytes=VMEM_LIMIT),
        name="xattn",
    )(x1_3d, kv, gx, w_xq, w_xo, gm, w_r, b_r)


def _router_kernel(lg_ref, ids_ref, gate_ref, cnt_ref, carry_ref):
    tm = lg_ref.shape[0]
    G, E = MOE_GROUPS, N_EXPERTS

    @pl.when(pl.program_id(0) == 0)
    def _():
        carry_ref[...] = jnp.zeros_like(carry_ref)

    lg = lg_ref[...]
    lane = lax.broadcasted_iota(I32, (tm, LANES), 1)
    lane_f = lane.astype(F32)
    ninf = -jnp.inf

    def first_lane(mask):
        return jnp.min(jnp.where(mask, lane_f, float(LANES)), axis=-1, keepdims=True).astype(I32)

    glm = jnp.where(lane < G, lg, ninf)
    gmax = jnp.max(glm, axis=-1, keepdims=True)
    g_idx = first_lane(glm == gmax)
    p_g = 1.0 / jnp.sum(jnp.exp(glm - gmax), axis=-1, keepdims=True)

    lo = G + g_idx * EXPERTS_PER_GROUP
    em = jnp.where((lane >= lo) & (lane < lo + EXPERTS_PER_GROUP), lg, ninf)
    top1 = jnp.max(em, axis=-1, keepdims=True)
    i1 = first_lane(em == top1)
    em2 = jnp.where(lane == i1, ninf, em)
    top2 = jnp.max(em2, axis=-1, keepdims=True)
    i2 = first_lane(em2 == top2)
    e2 = jnp.exp(top2 - top1)
    inv = 1.0 / (1.0 + e2)
    gate1 = p_g * inv
    gate2 = p_g * (e2 * inv)

    hot1 = lane == i1
    hot2 = lane == i2
    onehot = (hot1 | hot2).astype(BF16)
    r = lax.broadcasted_iota(I32, (tm, tm), 0)
    c = lax.broadcasted_iota(I32, (tm, tm), 1)
    strict = (r > c).astype(BF16)
    before = jnp.dot(strict, onehot, preferred_element_type=F32) + carry_ref[...]
    rank1 = jnp.sum(jnp.where(hot1, before, 0.0), axis=-1, keepdims=True)
    rank2 = jnp.sum(jnp.where(hot2, before, 0.0), axis=-1, keepdims=True)
    carry_ref[...] = carry_ref[...] + jnp.sum(onehot.astype(F32), axis=0, keepdims=True)

    ids = jnp.where(lane == 0, i1, 0)
    ids = jnp.where(lane == 1, i2, ids)
    ids = jnp.where(lane == 2, rank1.astype(I32), ids)
    ids = jnp.where(lane == 3, rank2.astype(I32), ids)
    ids_ref[...] = ids
    gate_ref[...] = jnp.where(lane == 0, gate1, jnp.where(lane == 1, gate2, 0.0))
    cnt_ref[...] = carry_ref[...]


def _router(logits, tm):
    n = logits.shape[0]
    return pl.pallas_call(
        _router_kernel,
        grid=(n // tm,),
        in_specs=[pl.BlockSpec((tm, LANES), lambda i: (i, 0))],
        out_specs=[
            pl.BlockSpec((tm, LANES), lambda i: (i, 0)),
            pl.BlockSpec((tm, LANES), lambda i: (i, 0)),
            pl.BlockSpec((1, LANES), lambda i: (0, 0)),
        ],
        out_shape=[
            jax.ShapeDtypeStruct((n, LANES), I32),
            jax.ShapeDtypeStruct((n, LANES), F32),
            jax.ShapeDtypeStruct((1, LANES), F32),
        ],
        scratch_shapes=[pltpu.VMEM((1, LANES), F32)],
        compiler_params=pltpu.CompilerParams(dimension_semantics=("arbitrary",)),
        name="router",
    )(logits)


def _plan_kernel(ids_ref, cnt_ref, dest_ref):
    tm = ids_ref.shape[0]
    ids = ids_ref[...]
    cnt = cnt_ref[...].astype(I32)
    padded = ((cnt + (ROW_BLOCK - 1)) >> ROW_BLOCK_BITS) << ROW_BLOCK_BITS
    k = lax.broadcasted_iota(I32, (LANES, LANES), 0)
    l = lax.broadcasted_iota(I32, (LANES, LANES), 1)
    before = (k < l).astype(BF16)
    pstart = _split3_dot(jnp.broadcast_to(padded.astype(F32), (8, LANES)), before, "at")[0:1]
    lane = lax.broadcasted_iota(I32, (tm, LANES), 1)

    def dest(col):
        start = jnp.sum(jnp.where(lane == ids[:, col:col + 1], pstart, 0.0), axis=-1, keepdims=True)
        return start + ids[:, col + 2:col + 3].astype(F32)

    d = jnp.where(lane == 0, dest(0), jnp.where(lane == 1, dest(1), 0.0))
    dest_ref[...] = d.T[0:8, :].astype(I32)


def _plan(ids, counts, tm):
    n = ids.shape[0]
    return pl.pallas_call(
        _plan_kernel,
        grid=(n // tm,),
        in_specs=[pl.BlockSpec((tm, LANES), lambda i: (i, 0)),
                  pl.BlockSpec((1, LANES), lambda i: (0, 0))],
        out_specs=pl.BlockSpec((8, tm), lambda i: (0, i)),
        out_shape=jax.ShapeDtypeStruct((8, n), I32),
        compiler_params=pltpu.CompilerParams(dimension_semantics=("arbitrary",)),
        name="plan",
    )(ids, counts)


def _dispatch_kernel(padstart_ref, padlen_ref, nvalid_ref, dest_hbm, x_ref, xs_hbm,
                     dsm, xcopy, zbuf, dsem, ssem, zsem):
    i = pl.program_id(0)
    nt = pl.num_programs(0)
    tm = x_ref.shape[0]
    slot = i % 2
    nb = xs_hbm.shape[0] // ROW_BLOCK

    def dest_copy(tile, k):
        buf = tile % 3
        return pltpu.make_async_copy(dest_hbm.at[k, pl.ds(tile * tm, tm)], dsm.at[buf, k],
                                     dsem.at[buf, k])

    def row_copy(r, k):
        return pltpu.make_async_copy(xcopy.at[slot, pl.ds(r, 1), :],
                                     xs_hbm.at[pl.ds(dsm[i % 3, k, r], 1), :], ssem.at[slot])

    def wait_rows(buf):
        for _ in range(MOE_TOPK):
            pltpu.make_async_copy(xcopy.at[buf], xs_hbm.at[pl.ds(0, tm), :], ssem.at[buf]).wait()

    def zero_copies(e):
        start = padstart_ref[e]
        n_pad = padlen_ref[e]
        head = jnp.minimum((-start) & 7, n_pad)
        for t in range(7):
            yield t < head, pltpu.make_async_copy(
                zbuf.at[pl.ds(0, 1), :], xs_hbm.at[pl.ds(start + t, 1), :], zsem)
        rest = n_pad - head
        for bit in range(3, ROW_BLOCK_BITS):
            size = 1 << bit
            off = pl.multiple_of(start + head + (rest & (size - 1)), 8)
            yield ((rest >> bit) & 1) == 1, pltpu.make_async_copy(
                zbuf.at[pl.ds(0, size), :], xs_hbm.at[pl.ds(off, size), :], zsem)

    def unused_block_copies():
        for blk in range(nb - N_EXPERTS, nb):
            yield blk >= nvalid_ref[0], pltpu.make_async_copy(
                zbuf, xs_hbm.at[pl.ds(blk * ROW_BLOCK, ROW_BLOCK), :], zsem)

    @pl.when(i == 0)
    def _():
        zbuf[...] = jnp.zeros_like(zbuf)
        for flag, cp in unused_block_copies():
            pl.when(flag)(cp.start)
        for e in range(N_EXPERTS):
            for flag, cp in zero_copies(e):
                pl.when(flag)(cp.start)
        for k in range(MOE_TOPK):
            dest_copy(0, k).start()

            @pl.when(nt > 1)
            def _():
                dest_copy(1, k).start()

    for k in range(MOE_TOPK):
        dest_copy(i, k).wait()

        @pl.when(i + 2 < nt)
        def _():
            dest_copy(i + 2, k).start()

    xcopy[slot] = x_ref[...]
    for r in range(tm):
        for k in range(MOE_TOPK):
            row_copy(r, k).start()

    @pl.when(i >= 1)
    def _():
        wait_rows(1 - slot)

    @pl.when(i == nt - 1)
    def _():
        wait_rows(slot)
        for flag, cp in unused_block_copies():
            pl.when(flag)(cp.wait)
        for e in range(N_EXPERTS):
            for flag, cp in zero_copies(e):
                pl.when(flag)(cp.wait)


def _dispatch(pad_start, pad_len, nvalid, dest_t, xn2, p_rows):
    n, d = xn2.shape
    tm = MOE_TILE
    grid_spec = pltpu.PrefetchScalarGridSpec(
        num_scalar_prefetch=3,
        grid=(n // tm,),
        in_specs=[
            pl.BlockSpec(memory_space=pl.ANY),
            pl.BlockSpec((tm, d), lambda i, ps, pn, nv: (i, 0)),
        ],
        out_specs=pl.BlockSpec(memory_space=pl.ANY),
        scratch_shapes=[
            pltpu.SMEM((3, MOE_TOPK, tm), I32),
            pltpu.VMEM((2, tm, d), F32),
            pltpu.VMEM((ROW_BLOCK, d), F32),
            pltpu.SemaphoreType.DMA((3, MOE_TOPK)),
            pltpu.SemaphoreType.DMA((2,)),
            pltpu.SemaphoreType.DMA,
        ],
    )
    return pl.pallas_call(
        _dispatch_kernel,
        grid_spec=grid_spec,
        out_shape=jax.ShapeDtypeStruct((p_rows, d), F32),
        compiler_params=pltpu.CompilerParams(
            dimension_semantics=("arbitrary",), vmem_limit_bytes=VMEM_LIMIT),
        name="dispatch",
    )(pad_start, pad_len, nvalid, dest_t, xn2)


def _ffn_kernel(bexp_ref, nvalid_ref, xs_ref, w1_ref, w3_ref, w2_ref, o_ref, w1b, w3b, w2b):
    i = pl.program_id(0)
    nvalid = nvalid_ref[0]

    @pl.when(i < nvalid)
    def _():
        changed = jnp.logical_or(i == 0, bexp_ref[i] != bexp_ref[jnp.maximum(i - 1, 0)])

        @pl.when(changed)
        def _():
            w1b[...] = w1_ref[0].astype(BF16)
            w3b[...] = w3_ref[0].astype(BF16)
            w2b[...] = w2_ref[0].astype(BF16)

        xb = xs_ref[...].astype(BF16)
        a = jnp.dot(xb, w1b[...], preferred_element_type=F32)
        g = jnp.dot(xb, w3b[...], preferred_element_type=F32)
        hb = (a * _sigmoid(a) * g).astype(BF16)
        o_ref[...] = jnp.dot(hb, w2b[...], preferred_element_type=F32)

    @pl.when(i >= nvalid)
    def _():
        o_ref[...] = jnp.zeros_like(o_ref)


def _ffn(block_expert, nvalid, xs, w1, w3, w2):
    p_rows, d = xs.shape
    tb = ROW_BLOCK
    _, _, f = w1.shape
    grid_spec = pltpu.PrefetchScalarGridSpec(
        num_scalar_prefetch=2,
        grid=(p_rows // tb,),
        in_specs=[
            pl.BlockSpec((tb, d), lambda i, be, nv: (jnp.minimum(i, nv[0] - 1), 0)),
            pl.BlockSpec((1, d, f), lambda i, be, nv: (be[i], 0, 0)),
            pl.BlockSpec((1, d, f), lambda i, be, nv: (be[i], 0, 0)),
            pl.BlockSpec((1, f, d), lambda i, be, nv: (be[i], 0, 0)),
        ],
        out_specs=pl.BlockSpec((tb, d), lambda i, be, nv: (i, 0)),
        scratch_shapes=[
            pltpu.VMEM((d, f), BF16),
            pltpu.VMEM((d, f), BF16),
            pltpu.VMEM((f, d), BF16),
        ],
    )
    return pl.pallas_call(
        _ffn_kernel,
        grid_spec=grid_spec,
        out_shape=jax.ShapeDtypeStruct((p_rows, d), F32),
        compiler_params=pltpu.CompilerParams(
            dimension_semantics=("arbitrary",), vmem_limit_bytes=VMEM_LIMIT),
        name="ffn",
    )(block_expert, nvalid, xs, w1, w3, w2)


def _combine_kernel(dest_hbm, y_hbm, x2_ref, gate_ref, g_ref, out_ref, dsm, ybuf, dsem, gsem):
    j = pl.program_id(0)
    nt = pl.num_programs(0) - 1
    tm = x2_ref.shape[0]

    def dest_copy(tile, k):
        buf = tile % 3
        return pltpu.make_async_copy(dest_hbm.at[k, pl.ds(tile * tm, tm)], dsm.at[buf, k],
                                     dsem.at[buf, k])

    def row_copy(r, k):
        return pltpu.make_async_copy(y_hbm.at[pl.ds(dsm[j % 3, k, r], 1), :],
                                     ybuf.at[j % 2, k, pl.ds(r, 1), :], gsem.at[j % 2])

    @pl.when(j == 0)
    def _():
        for k in range(MOE_TOPK):
            dest_copy(0, k).start()

    @pl.when(j < nt)
    def _():
        for k in range(MOE_TOPK):
            dest_copy(j, k).wait()

            @pl.when(j + 1 < nt)
            def _():
                dest_copy(j + 1, k).start()

        for r in range(tm):
            for k in range(MOE_TOPK):
                row_copy(r, k).start()

    @pl.when(j >= 1)
    def _():
        buf = (j - 1) % 2
        for k in range(MOE_TOPK):
            pltpu.make_async_copy(y_hbm.at[pl.ds(0, tm), :], ybuf.at[buf, k], gsem.at[buf]).wait()
        gate = gate_ref[...]
        x = x2_ref[...] + gate[:, 0:1] * ybuf[buf, 0] + gate[:, 1:2] * ybuf[buf, 1]
        out_ref[...] = _rms(x, g_ref[...])


def _combine(dest_t, y, x2, gates, g):
    n, d = x2.shape
    tm = MOE_TILE
    nt = n // tm
    prev = lambda j: (jnp.maximum(j - 1, 0), 0)
    return pl.pallas_call(
        _combine_kernel,
        grid=(nt + 1,),
        in_specs=[
            pl.BlockSpec(memory_space=pl.ANY),
            pl.BlockSpec(memory_space=pl.ANY),
            pl.BlockSpec((tm, d), prev),
            pl.BlockSpec((tm, LANES), prev),
            pl.BlockSpec((1, d), lambda j: (0, 0)),
        ],
        out_specs=pl.BlockSpec((tm, d), prev),
        out_shape=jax.ShapeDtypeStruct((n, d), F32),
        scratch_shapes=[
            pltpu.SMEM((3, MOE_TOPK, tm), I32),
            pltpu.VMEM((2, MOE_TOPK, tm, d), F32),
            pltpu.SemaphoreType.DMA((3, MOE_TOPK)),
            pltpu.SemaphoreType.DMA((2,)),
        ],
        compiler_params=pltpu.CompilerParams(
            dimension_semantics=("arbitrary",), vmem_limit_bytes=VMEM_LIMIT),
        name="combine",
    )(dest_t, y, x2, gates, g)


def kernel(x, mem, norm_mix_g, w_in, b_gate, b_if, conv_w, conv_b, mh_norm_g, sgu_norm_g, w_s, b_s, w_out, norm_x_g, norm_mem_g, w_xq, w_xkv, w_xo, norm_moe_g, w_rg, b_rg, w_re, b_re, w1, w3, w2, norm_f_g):
    assert w_in.shape[0] == 1, "the combine stage fuses the final norm: one layer only"
    (norm_mix_g, w_in, b_gate, b_if, conv_w, conv_b, mh_norm_g, sgu_norm_g, w_s, b_s, w_out, norm_x_g,
     norm_mem_g, w_xq, w_xkv, w_xo, norm_moe_g, w_rg, b_rg, w_re, b_re, w1, w3, w2) = (
        p[0] for p in (norm_mix_g, w_in, b_gate, b_if, conv_w, conv_b, mh_norm_g, sgu_norm_g, w_s, b_s,
                       w_out, norm_x_g, norm_mem_g, w_xq, w_xkv, w_xo, norm_moe_g, w_rg, b_rg, w_re,
                       b_re, w1, w3, w2))
    b, s, d = x.shape
    n = b * s
    H = MLSTM_HEADS
    x2d = x.reshape(n, d)
    row = lambda v: v.reshape(1, -1)

    c_if = 4 * d
    c_u = c_if + 2 * H
    w_qk = w_in[:, 0:2 * d].astype(BF16)
    w_vo = w_in[:, 2 * d:c_if].astype(BF16)
    w_if = jnp.pad(w_in[:, c_if:c_u], ((0, 0), (0, LANES - 2 * H))).astype(BF16)
    w_usv = w_in[:, c_u:c_u + 2 * d].astype(BF16)
    w_gab = w_in[:, c_u + 2 * d:].astype(BF16)
    b_if_row = jnp.pad(b_if, (0, LANES - 2 * H)).reshape(1, LANES)

    tm = min(1024, s)
    g_mix = row(norm_mix_g)
    wide = jax.ShapeDtypeStruct((n, 2 * d), BF16)
    qk_body = functools.partial(_proj_qk_kernel, tiles_per_seq=s // tm)
    (qk,) = _proj_call(qk_body, "proj_qk", x2d, g_mix, w_qk, [conv_w, row(conv_b)], [wide],
                       [pltpu.VMEM((CONV_HALO, 2 * d), F32)], tm)
    (vo,) = _proj_call(_proj_vo_kernel, "proj_vo", x2d, g_mix, w_vo, [], [wide], [], tm)
    (usv,) = _proj_call(_proj_sgu_kernel, "proj_sgu", x2d, g_mix, w_usv, [row(sgu_norm_g)], [wide],
                        [], tm)
    gab, gc = _proj_call(_proj_gate_kernel, "proj_gate", x2d, g_mix, w_gab,
                         [row(b_gate), w_if, b_if_row],
                         [wide, jax.ShapeDtypeStruct((n, LANES), F32)], [], tm)

    ya = _mlstm(qk.reshape(b, s, -1), vo.reshape(b, s, -1), gc.reshape(b, s, LANES), row(mh_norm_g))
    x1 = _merge(usv, gab, ya.reshape(n, d), x2d, w_s, b_s.T, w_out.astype(BF16), tm=min(512, n))

    kv = _memkv(mem, row(norm_mem_g), w_xkv.astype(BF16))
    n_route = MOE_GROUPS + N_EXPERTS
    w_r = jnp.pad(jnp.concatenate([w_rg, w_re], axis=1), ((0, 0), (0, LANES - n_route)))
    b_r = jnp.pad(jnp.concatenate([b_rg, b_re]), (0, LANES - n_route)).reshape(1, LANES)
    x2, xn2, logits = _xattn(x1.reshape(b, s, d), kv, row(norm_x_g), w_xq.astype(BF16),
                             w_xo.astype(BF16), row(norm_moe_g), w_r, b_r, tm=min(512, s))
    x2 = x2.reshape(n, d)
    xn2 = xn2.reshape(n, d)

    ids, gates, counts = _router(logits.reshape(n, LANES), tm=min(512, n))
    dest_t = _plan(ids, counts, tm=min(512, n))

    tb = ROW_BLOCK
    nb = n * MOE_TOPK // tb + N_EXPERTS
    cnt = counts[0, MOE_GROUPS:n_route].astype(I32)
    padded = (cnt + tb - 1) // tb * tb
    pends = jnp.cumsum(padded)
    block_start = jnp.arange(nb, dtype=I32) * tb
    block_expert = jnp.minimum(jnp.sum(pends[None, :] <= block_start[:, None], axis=1),
                               N_EXPERTS - 1).astype(I32)
    nvalid = (pends[-1] // tb).astype(I32).reshape(1)
    pad_start = (pends - padded + cnt).astype(I32)
    pad_len = (padded - cnt).astype(I32)

    xs = _dispatch(pad_start, pad_len, nvalid, dest_t, xn2, nb * tb)
    y = _ffn(block_expert, nvalid, xs, w1, w3, w2)
    return _combine(dest_t, y, x2, gates, row(norm_f_g)).reshape(b, s, d)
```

```python
import jax
import jax.numpy as jnp
from jax import lax
from jax.experimental import pallas as pl
from jax.experimental.pallas import tpu as pltpu

F32 = jnp.float32
BF16 = jnp.bfloat16
I32 = jnp.int32
U32 = jnp.uint32

EPS = 1e-6
LANES = 128
CHUNK = 128
MLSTM_HEADS = 4
SGU_GROUPS = 8
XATTN_HEADS = 4
MOE_GROUPS = 4
EXPERTS_PER_GROUP = 8
N_EXPERTS = MOE_GROUPS * EXPERTS_PER_GROUP
MOE_TOPK = 2
CONV_WIDTH = 4
CONV_HALO = 8
ROW_BLOCK = 256
ROW_BLOCK_BITS = ROW_BLOCK.bit_length() - 1
MOE_TILE = 256
VMEM_LIMIT = 56 * 1024 * 1024


def _sigmoid(x):
    return 0.5 * jnp.tanh(0.5 * x) + 0.5


def _rms(x, g):
    return x * lax.rsqrt(jnp.mean(x * x, axis=-1, keepdims=True) + EPS) * g


def _log_sigmoid(x):
    return jnp.minimum(x, 0.0) - jnp.log(1.0 + jnp.exp(-jnp.abs(x)))


def _gelu_tanh(x):
    c = 0.7978845608028654
    return 0.5 * x * (1.0 + jnp.tanh(c * (x + 0.044715 * (x * x * x))))


def _pack_halves(x):
    n = x.shape[1] // 2
    lo = lax.bitcast_convert_type(x[:, 0:n].astype(BF16).astype(F32), U32)
    hi = lax.bitcast_convert_type(x[:, n:2 * n].astype(BF16).astype(F32), U32)
    return (hi & jnp.uint32(0xFFFF0000)) | (lo >> 16)


def _unpack_halves(p):
    lo = lax.bitcast_convert_type(p << 16, F32)
    hi = lax.bitcast_convert_type(p & jnp.uint32(0xFFFF0000), F32)
    return lo, hi


def _split3_dot(a, tri, dims):
    a_hi = a.astype(BF16)
    r1 = a - a_hi.astype(F32)
    a_mid = r1.astype(BF16)
    a_lo = (r1 - a_mid.astype(F32)).astype(BF16)
    if dims == "at":
        f = lambda p: jnp.dot(p, tri, preferred_element_type=F32)
    else:
        f = lambda p: jnp.dot(tri, p, preferred_element_type=F32)
    return f(a_hi) + f(a_mid) + f(a_lo)


def _split3_both(a, b):
    a_hi = a.astype(BF16)
    a_lo = (a - a_hi.astype(F32)).astype(BF16)
    b_hi = b.astype(BF16)
    b_lo = (b - b_hi.astype(F32)).astype(BF16)
    f = lambda p, r: jnp.dot(p, r, preferred_element_type=F32)
    return f(a_hi, b_hi) + (f(a_hi, b_lo) + f(a_lo, b_hi))


def _inproj_kernel(x_ref, g_ref, w_ref, wif_ref, bc_ref, proj_ref, gc_ref):
    xn = _rms(x_ref[...], g_ref[...]).astype(BF16)
    proj_ref[...] = jnp.dot(xn, w_ref[...], preferred_element_type=F32).astype(BF16)

    @pl.when(pl.program_id(1) == 0)
    def _():
        gc_ref[...] = jnp.dot(xn, wif_ref[...], preferred_element_type=F32) + bc_ref[...]


def _inproj(x2d, g, w_main, w_if, b_col, tm, tn):
    n, d = x2d.shape
    nc = w_main.shape[1]
    return pl.pallas_call(
        _inproj_kernel,
        grid=(n // tm, nc // tn),
        in_specs=[
            pl.BlockSpec((tm, d), lambda i, j: (i, 0)),
            pl.BlockSpec((1, d), lambda i, j: (0, 0)),
            pl.BlockSpec((d, tn), lambda i, j: (0, j)),
            pl.BlockSpec((d, LANES), lambda i, j: (0, 0)),
            pl.BlockSpec((1, LANES), lambda i, j: (0, 0)),
        ],
        out_specs=[
            pl.BlockSpec((tm, tn), lambda i, j: (i, j)),
            pl.BlockSpec((tm, LANES), lambda i, j: (i, 0)),
        ],
        out_shape=[
            jax.ShapeDtypeStruct((n, nc), BF16),
            jax.ShapeDtypeStruct((n, LANES), F32),
        ],
        compiler_params=pltpu.CompilerParams(
            dimension_semantics=("arbitrary", "arbitrary"), vmem_limit_bytes=VMEM_LIMIT),
        name="inproj",
    )(x2d, g, w_main, w_if, b_col)


def _mlstm_kernel(qk_ref, v_ref, o_ref, gc_ref, cw_ref, cb_ref, mhg_ref, ya_ref,
                  tail_ref, c_ref, m_ref):
    L = CHUNK
    H = MLSTM_HEADS
    n_seq, _, dqk = qk_ref.shape
    dh = dqk // (2 * H)

    @pl.when(pl.program_id(0) == 0)
    def _():
        tail_ref[...] = jnp.zeros_like(tail_ref)
        c_ref[...] = jnp.zeros_like(c_ref)
        m_ref[...] = jnp.zeros_like(m_ref)

    row = lax.broadcasted_iota(I32, (L, L), 0)
    col = lax.broadcasted_iota(I32, (L, L), 1)
    causal = row >= col
    tril = causal.astype(BF16)
    triu = (row <= col).astype(BF16)
    results = []
    for bi in range(n_seq):
        results += _mlstm_chunk(bi, qk_ref, v_ref, o_ref, gc_ref, cw_ref, cb_ref, mhg_ref,
                                tail_ref, c_ref, m_ref, causal, tril, triu, dh)
    for si, c_new, m_new, y in results:
        bi, h = divmod(si, H)
        c_ref[si] = c_new
        m_ref[si:si + 1, :] = jnp.broadcast_to(m_new, (1, LANES))
        ya_ref[bi, :, h * dh:(h + 1) * dh] = y


def _mlstm_chunk(bi, qk_ref, v_ref, o_ref, gc_ref, cw_ref, cb_ref, mhg_ref, tail_ref, c_ref, m_ref,
                 causal, tril, triu, dh):
    L = CHUNK
    H = MLSTM_HEADS
    dqk = 2 * H * dh
    gc = gc_ref[bi]
    gr = gc.T[0:8, :]
    b_c = _split3_dot(_log_sigmoid(gc), tril, "ta")
    b_r = _split3_dot(_log_sigmoid(gr), triu, "at")

    cur = qk_ref[bi].astype(F32)
    tail = tail_ref[bi]
    row8 = lax.broadcasted_iota(I32, (CONV_HALO, dqk), 0)
    acc = cb_ref[...] + cw_ref[CONV_WIDTH - 1:CONV_WIDTH, :] * cur
    for k in range(1, CONV_WIDTH):
        rolled = pltpu.roll(cur, k, axis=0)
        head = jnp.where(row8 < k, pltpu.roll(tail, k, axis=0), rolled[0:CONV_HALO])
        shifted = jnp.concatenate([head, rolled[CONV_HALO:]], axis=0)
        acc = acc + cw_ref[CONV_WIDTH - 1 - k:CONV_WIDTH - k, :] * shifted
    tail_ref[bi] = cur[L - CONV_HALO:L]
    half = 0.5 * acc
    act = half + half * jnp.tanh(half)
    ones_col = (lax.broadcasted_iota(I32, (L, LANES), 1) == 0).astype(BF16)
    q_all = act[:, 0:H * dh]
    k_all = act[:, H * dh:2 * H * dh] * (dh ** -0.5)
    qb_all = q_all.astype(BF16)
    kb_all = k_all.astype(BF16)

    results = []
    for h in range(H):
        si = bi * H + h
        q = q_all[:, h * dh:(h + 1) * dh]
        k = k_all[:, h * dh:(h + 1) * dh]
        qb = qb_all[:, h * dh:(h + 1) * dh]
        kb = kb_all[:, h * dh:(h + 1) * dh]
        vaug = jnp.concatenate([v_ref[bi, :, h * dh:(h + 1) * dh], ones_col], axis=-1)
        bcol = b_c[:, H + h:H + h + 1]
        icol = gc[:, h:h + 1]
        brow = b_r[H + h:H + h + 1, :]
        irow = gr[h:h + 1, :]
        m_prev = m_ref[si:si + 1, 0:1]

        dmat = jnp.where(causal, bcol - brow + irow, -jnp.inf)
        inter = bcol + m_prev
        m_t = jnp.maximum(inter, jnp.max(dmat, axis=-1, keepdims=True))
        dexp = jnp.exp(dmat - m_t)
        inter_w = jnp.exp(inter - m_t)
        s = lax.dot_general(qb, kb, (((1,), (1,)), ((), ())), preferred_element_type=F32) * dexp
        c_old = c_ref[si]
        out = (jnp.dot(s.astype(BF16), vaug, preferred_element_type=F32)
               + inter_w * jnp.dot(qb, c_old.astype(BF16), preferred_element_type=F32))
        den = out[:, dh:dh + 1]
        hh = out[:, 0:dh] * (1.0 / jnp.maximum(jnp.abs(den), jnp.exp(-m_t)))

        b_last = bcol[L - 1:L, :]
        wlog = b_last - bcol + icol
        m_new = jnp.maximum(b_last + m_prev, jnp.max(wlog, axis=0, keepdims=True))
        decay = jnp.exp(b_last + m_prev - m_new)
        kw = (k * jnp.exp(wlog - m_new)).astype(BF16)
        c_new = decay * c_old + lax.dot_general(kw, vaug, (((0,), (0,)), ((), ())),
                                                preferred_element_type=F32)
        y = _rms(hh, mhg_ref[:, h * dh:(h + 1) * dh])
        og = _sigmoid(o_ref[bi, :, h * dh:(h + 1) * dh].astype(F32))
        results.append((si, c_new, m_new, (y * og).astype(BF16)))
    return results


def _mlstm(proj3, gc3, conv_w, conv_b, mh_g, d):
    b, s, _ = proj3.shape
    nc = s // CHUNK
    dqk = 2 * d
    dh = d // MLSTM_HEADS
    n_state = b * MLSTM_HEADS
    return pl.pallas_call(
        _mlstm_kernel,
        grid=(nc,),
        in_specs=[
            pl.BlockSpec((b, CHUNK, dqk), lambda ci: (0, ci, 0)),
            pl.BlockSpec((b, CHUNK, d), lambda ci: (0, ci, 2)),
            pl.BlockSpec((b, CHUNK, d), lambda ci: (0, ci, 3)),
            pl.BlockSpec((b, CHUNK, LANES), lambda ci: (0, ci, 0)),
            pl.BlockSpec((CONV_WIDTH, dqk), lambda ci: (0, 0)),
            pl.BlockSpec((1, dqk), lambda ci: (0, 0)),
            pl.BlockSpec((1, d), lambda ci: (0, 0)),
        ],
        out_specs=pl.BlockSpec((b, CHUNK, d), lambda ci: (0, ci, 0)),
        out_shape=jax.ShapeDtypeStruct((b, s, d), BF16),
        scratch_shapes=[
            pltpu.VMEM((b, CONV_HALO, dqk), F32),
            pltpu.VMEM((n_state, dh, dh + LANES), F32),
            pltpu.VMEM((n_state, LANES), F32),
        ],
        compiler_params=pltpu.CompilerParams(
            dimension_semantics=("arbitrary",), vmem_limit_bytes=VMEM_LIMIT),
        name="mlstm",
    )(proj3, proj3, proj3, gc3, conv_w, conv_b, mh_g)


def _merge_kernel(u_ref, sv_ref, ga_ref, gb_ref, ya_ref, x_ref, ws_ref, bst_ref, sg_ref, bg_ref,
                  wout_ref, x1_ref):
    tm, d = x_ref.shape
    L = CHUNK
    G = SGU_GROUPS
    gd = d // G
    uu = _gelu_tanh(u_ref[...].astype(F32))
    vn = _rms(_gelu_tanh(sv_ref[...].astype(F32)), sg_ref[...]).astype(BF16)
    row = lax.broadcasted_iota(I32, (L, L), 0)
    col = lax.broadcasted_iota(I32, (L, L), 1)
    causal = row >= col
    cols = []
    for g in range(G):
        wg = jnp.where(causal, ws_ref[g], 0.0).astype(BF16)
        bias = bst_ref[:, g:g + 1]
        rows = [jnp.dot(wg, vn[ci * L:(ci + 1) * L, g * gd:(g + 1) * gd],
                        preferred_element_type=F32) + bias for ci in range(tm // L)]
        cols.append(jnp.concatenate(rows, axis=0))
    yb = uu * jnp.concatenate(cols, axis=1)
    g_a = _sigmoid(ga_ref[...].astype(F32) + bg_ref[:, 0:d])
    g_b = _sigmoid(gb_ref[...].astype(F32) + bg_ref[:, d:2 * d])
    merged = (g_a * ya_ref[...].astype(F32) + g_b * yb).astype(BF16)
    x1_ref[...] = x_ref[...] + jnp.dot(merged, wout_ref[...], preferred_element_type=F32)


def _merge(proj, ya, x2d, w_s, b_st, sgu_g, b_gate, w_out, tm):
    n, d = x2d.shape
    full = lambda shape: pl.BlockSpec(shape, lambda i: (0,) * len(shape))
    return pl.pallas_call(
        _merge_kernel,
        grid=(n // tm,),
        in_specs=[
            pl.BlockSpec((tm, d), lambda i: (i, 4)),
            pl.BlockSpec((tm, d), lambda i: (i, 5)),
            pl.BlockSpec((tm, d), lambda i: (i, 6)),
            pl.BlockSpec((tm, d), lambda i: (i, 7)),
            pl.BlockSpec((tm, d), lambda i: (i, 0)),
            pl.BlockSpec((tm, d), lambda i: (i, 0)),
            full(w_s.shape), full(b_st.shape), full(sgu_g.shape), full(b_gate.shape), full(w_out.shape),
        ],
        out_specs=pl.BlockSpec((tm, d), lambda i: (i, 0)),
        out_shape=jax.ShapeDtypeStruct((n, d), F32),
        compiler_params=pltpu.CompilerParams(
            dimension_semantics=("arbitrary",), vmem_limit_bytes=VMEM_LIMIT),
        name="merge",
    )(proj, proj, proj, proj, ya, x2d, w_s, b_st, sgu_g, b_gate, w_out)


def _memkv_kernel(mem_ref, g_ref, w_ref, kv_ref):
    mn = _rms(mem_ref[0], g_ref[...]).astype(BF16)
    kv_ref[0] = jnp.dot(mn, w_ref[...], preferred_element_type=F32).astype(BF16)


def _memkv(mem, g, w_xkv):
    b, m, d = mem.shape
    dk = w_xkv.shape[1]
    return pl.pallas_call(
        _memkv_kernel,
        grid=(b,),
        in_specs=[
            pl.BlockSpec((1, m, d), lambda i: (i, 0, 0)),
            pl.BlockSpec((1, d), lambda i: (0, 0)),
            pl.BlockSpec((d, dk), lambda i: (0, 0)),
        ],
        out_specs=pl.BlockSpec((1, m, dk), lambda i: (i, 0, 0)),
        out_shape=jax.ShapeDtypeStruct((b, m, dk), BF16),
        compiler_params=pltpu.CompilerParams(
            dimension_semantics=("arbitrary",), vmem_limit_bytes=VMEM_LIMIT),
        name="memkv",
    )(mem, g, w_xkv)


def _xattn_kernel(x1_ref, kv_ref, gx_ref, wq_ref, wo_ref, gm_ref, wr_ref, br_ref,
                  x2_ref, xn2_ref, lg_ref, att_ref):
    tm, d = x1_ref.shape[1:]
    H = XATTN_HEADS
    dh = d // H
    x1 = x1_ref[0]
    xn = _rms(x1, gx_ref[...]).astype(BF16)
    q = (jnp.dot(xn, wq_ref[...], preferred_element_type=F32) * (dh ** -0.5)).astype(BF16)
    for h in range(H):
        kh = kv_ref[0, :, h * dh:(h + 1) * dh]
        vh = kv_ref[0, :, d + h * dh:d + (h + 1) * dh]
        s = lax.dot_general(q[:, h * dh:(h + 1) * dh], kh, (((1,), (1,)), ((), ())),
                            preferred_element_type=F32)
        e = jnp.exp(s - jnp.max(s, axis=-1, keepdims=True))
        p = e * (1.0 / jnp.sum(e, axis=-1, keepdims=True))
        att_ref[:, h * dh:(h + 1) * dh] = jnp.dot(p.astype(BF16), vh,
                                                 preferred_element_type=F32).astype(BF16)
    x2 = x1 + jnp.dot(att_ref[...], wo_ref[...], preferred_element_type=F32)
    x2_ref[0] = x2
    xn2 = _rms(x2, gm_ref[...])
    xn2_ref[0] = _pack_halves(xn2)
    lg_ref[0] = _split3_both(xn2, wr_ref[...]) + br_ref[...]


def _xattn(x1_3d, kv, gx, w_xq, w_xo, gm, w_r, b_r, tm):
    b, s, d = x1_3d.shape
    m = kv.shape[1]
    full = lambda shape: pl.BlockSpec(shape, lambda bi, i: (0,) * len(shape))
    return pl.pallas_call(
        _xattn_kernel,
        grid=(b, s // tm),
        in_specs=[
            pl.BlockSpec((1, tm, d), lambda bi, i: (bi, i, 0)),
            pl.BlockSpec((1, m, 2 * d), lambda bi, i: (bi, 0, 0)),
            full(gx.shape), full(w_xq.shape), full(w_xo.shape), full(gm.shape), full(w_r.shape),
            full(b_r.shape),
        ],
        out_specs=[
            pl.BlockSpec((1, tm, d), lambda bi, i: (bi, i, 0)),
            pl.BlockSpec((1, tm, d // 2), lambda bi, i: (bi, i, 0)),
            pl.BlockSpec((1, tm, LANES), lambda bi, i: (bi, i, 0)),
        ],
        out_shape=[
            jax.ShapeDtypeStruct((b, s, d), F32),
            jax.ShapeDtypeStruct((b, s, d // 2), U32),
            jax.ShapeDtypeStruct((b, s, LANES), F32),
        ],
        scratch_shapes=[pltpu.VMEM((tm, d), BF16)],
        compiler_params=pltpu.CompilerParams(
            dimension_semantics=("arbitrary", "arbitrary"), vmem_limit_bytes=VMEM_LIMIT),
        name="xattn",
    )(x1_3d, kv, gx, w_xq, w_xo, gm, w_r, b_r)


def _router_kernel(lg_ref, ids_ref, gate_ref, cnt_ref, carry_ref):
    tm = lg_ref.shape[0]
    G, E = MOE_GROUPS, N_EXPERTS

    @pl.when(pl.program_id(0) == 0)
    def _():
        carry_ref[...] = jnp.zeros_like(carry_ref)

    lg = lg_ref[...]
    lane = lax.broadcasted_iota(I32, (tm, LANES), 1)
    lane_f = lane.astype(F32)
    ninf = -jnp.inf

    def first_lane(mask):
        return jnp.min(jnp.where(mask, lane_f, float(LANES)), axis=-1, keepdims=True).astype(I32)

    glm = jnp.where(lane < G, lg, ninf)
    gmax = jnp.max(glm, axis=-1, keepdims=True)
    g_idx = first_lane(glm == gmax)
    p_g = 1.0 / jnp.sum(jnp.exp(glm - gmax), axis=-1, keepdims=True)

    lo = G + g_idx * EXPERTS_PER_GROUP
    em = jnp.where((lane >= lo) & (lane < lo + EXPERTS_PER_GROUP), lg, ninf)
    top1 = jnp.max(em, axis=-1, keepdims=True)
    i1 = first_lane(em == top1)
    em2 = jnp.where(lane == i1, ninf, em)
    top2 = jnp.max(em2, axis=-1, keepdims=True)
    i2 = first_lane(em2 == top2)
    e2 = jnp.exp(top2 - top1)
    inv = 1.0 / (1.0 + e2)
    gate1 = p_g * inv
    gate2 = p_g * (e2 * inv)

    hot1 = lane == i1
    hot2 = lane == i2
    onehot = (hot1 | hot2).astype(BF16)
    r = lax.broadcasted_iota(I32, (tm, tm), 0)
    c = lax.broadcasted_iota(I32, (tm, tm), 1)
    strict = (r > c).astype(BF16)
    before = jnp.dot(strict, onehot, preferred_element_type=F32) + carry_ref[...]
    rank1 = jnp.sum(jnp.where(hot1, before, 0.0), axis=-1, keepdims=True)
    rank2 = jnp.sum(jnp.where(hot2, before, 0.0), axis=-1, keepdims=True)
    carry_ref[...] = carry_ref[...] + jnp.sum(onehot.astype(F32), axis=0, keepdims=True)

    ids = jnp.where(lane == 0, i1, 0)
    ids = jnp.where(lane == 1, i2, ids)
    ids = jnp.where(lane == 2, rank1.astype(I32), ids)
    ids = jnp.where(lane == 3, rank2.astype(I32), ids)
    ids_ref[...] = ids
    gate_ref[...] = jnp.where(lane == 0, gate1, jnp.where(lane == 1, gate2, 0.0))
    cnt_ref[...] = carry_ref[...]


def _router(logits, tm):
    n = logits.shape[0]
    return pl.pallas_call(
        _router_kernel,
        grid=(n // tm,),
        in_specs=[pl.BlockSpec((tm, LANES), lambda i: (i, 0))],
        out_specs=[
            pl.BlockSpec((tm, LANES), lambda i: (i, 0)),
            pl.BlockSpec((tm, LANES), lambda i: (i, 0)),
            pl.BlockSpec((1, LANES), lambda i: (0, 0)),
        ],
        out_shape=[
            jax.ShapeDtypeStruct((n, LANES), I32),
            jax.ShapeDtypeStruct((n, LANES), F32),
            jax.ShapeDtypeStruct((1, LANES), F32),
        ],
        scratch_shapes=[pltpu.VMEM((1, LANES), F32)],
        compiler_params=pltpu.CompilerParams(dimension_semantics=("arbitrary",)),
        name="router",
    )(logits)


def _plan_kernel(ids_ref, cnt_ref, dest_ref):
    tm = ids_ref.shape[0]
    ids = ids_ref[...]
    cnt = cnt_ref[...].astype(I32)
    padded = ((cnt + (ROW_BLOCK - 1)) >> ROW_BLOCK_BITS) << ROW_BLOCK_BITS
    k = lax.broadcasted_iota(I32, (LANES, LANES), 0)
    l = lax.broadcasted_iota(I32, (LANES, LANES), 1)
    before = (k < l).astype(BF16)
    pstart = _split3_dot(jnp.broadcast_to(padded.astype(F32), (8, LANES)), before, "at")[0:1]
    lane = lax.broadcasted_iota(I32, (tm, LANES), 1)

    def dest(col):
        start = jnp.sum(jnp.where(lane == ids[:, col:col + 1], pstart, 0.0), axis=-1, keepdims=True)
        return start + ids[:, col + 2:col + 3].astype(F32)

    d = jnp.where(lane == 0, dest(0), jnp.where(lane == 1, dest(1), 0.0))
    dest_ref[...] = d.T[0:8, :].astype(I32)


def _plan(ids, counts, tm):
    n = ids.shape[0]
    return pl.pallas_call(
        _plan_kernel,
        grid=(n // tm,),
        in_specs=[pl.BlockSpec((tm, LANES), lambda i: (i, 0)),
                  pl.BlockSpec((1, LANES), lambda i: (0, 0))],
        out_specs=pl.BlockSpec((8, tm), lambda i: (0, i)),
        out_shape=jax.ShapeDtypeStruct((8, n), I32),
        compiler_params=pltpu.CompilerParams(dimension_semantics=("arbitrary",)),
        name="plan",
    )(ids, counts)


def _dispatch_kernel(padstart_ref, padlen_ref, nvalid_ref, dest_hbm, x_ref, xs_hbm,
                     dsm, xcopy, zbuf, dsem, ssem, zsem):
    i = pl.program_id(0)
    nt = pl.num_programs(0)
    tm = x_ref.shape[0]
    slot = i % 2
    nb = xs_hbm.shape[0] // ROW_BLOCK

    def dest_copy(tile, k):
        buf = tile % 3
        return pltpu.make_async_copy(dest_hbm.at[k, pl.ds(tile * tm, tm)], dsm.at[buf, k],
                                     dsem.at[buf, k])

    def row_copy(r, k):
        return pltpu.make_async_copy(xcopy.at[slot, pl.ds(r, 1), :],
                                     xs_hbm.at[pl.ds(dsm[i % 3, k, r], 1), :], ssem.at[slot])

    def wait_rows(buf):
        for _ in range(MOE_TOPK):
            pltpu.make_async_copy(xcopy.at[buf], xs_hbm.at[pl.ds(0, tm), :], ssem.at[buf]).wait()

    def zero_copies(e):
        start = padstart_ref[e]
        n_pad = padlen_ref[e]
        head = jnp.minimum((-start) & 7, n_pad)
        for t in range(7):
            yield t < head, pltpu.make_async_copy(
                zbuf.at[pl.ds(0, 1), :], xs_hbm.at[pl.ds(start + t, 1), :], zsem)
        rest = n_pad - head
        for bit in range(3, ROW_BLOCK_BITS):
            size = 1 << bit
            off = pl.multiple_of(start + head + (rest & (size - 1)), 8)
            yield ((rest >> bit) & 1) == 1, pltpu.make_async_copy(
                zbuf.at[pl.ds(0, size), :], xs_hbm.at[pl.ds(off, size), :], zsem)

    def unused_block_copies():
        for blk in range(nb - N_EXPERTS, nb):
            yield blk >= nvalid_ref[0], pltpu.make_async_copy(
                zbuf, xs_hbm.at[pl.ds(blk * ROW_BLOCK, ROW_BLOCK), :], zsem)

    @pl.when(i == 0)
    def _():
        zbuf[...] = jnp.zeros_like(zbuf)
        for flag, cp in unused_block_copies():
            pl.when(flag)(cp.start)
        for e in range(N_EXPERTS):
            for flag, cp in zero_copies(e):
                pl.when(flag)(cp.start)
        for k in range(MOE_TOPK):
            dest_copy(0, k).start()

            @pl.when(nt > 1)
            def _():
                dest_copy(1, k).start()

    for k in range(MOE_TOPK):
        dest_copy(i, k).wait()

        @pl.when(i + 2 < nt)
        def _():
            dest_copy(i + 2, k).start()

    xcopy[slot] = x_ref[...]
    for r in range(tm):
        for k in range(MOE_TOPK):
            row_copy(r, k).start()

    @pl.when(i >= 1)
    def _():
        wait_rows(1 - slot)

    @pl.when(i == nt - 1)
    def _():
        wait_rows(slot)
        for flag, cp in unused_block_copies():
            pl.when(flag)(cp.wait)
        for e in range(N_EXPERTS):
            for flag, cp in zero_copies(e):
                pl.when(flag)(cp.wait)


def _dispatch(pad_start, pad_len, nvalid, dest_t, xn2, p_rows):
    n, d = xn2.shape
    tm = MOE_TILE
    grid_spec = pltpu.PrefetchScalarGridSpec(
        num_scalar_prefetch=3,
        grid=(n // tm,),
        in_specs=[
            pl.BlockSpec(memory_space=pl.ANY),
            pl.BlockSpec((tm, d), lambda i, ps, pn, nv: (i, 0)),
        ],
        out_specs=pl.BlockSpec(memory_space=pl.ANY),
        scratch_shapes=[
            pltpu.SMEM((3, MOE_TOPK, tm), I32),
            pltpu.VMEM((2, tm, d), xn2.dtype),
            pltpu.VMEM((ROW_BLOCK, d), xn2.dtype),
            pltpu.SemaphoreType.DMA((3, MOE_TOPK)),
            pltpu.SemaphoreType.DMA((2,)),
            pltpu.SemaphoreType.DMA,
        ],
    )
    return pl.pallas_call(
        _dispatch_kernel,
        grid_spec=grid_spec,
        out_shape=jax.ShapeDtypeStruct((p_rows, d), xn2.dtype),
        compiler_params=pltpu.CompilerParams(
            dimension_semantics=("arbitrary",), vmem_limit_bytes=VMEM_LIMIT),
        name="dispatch",
    )(pad_start, pad_len, nvalid, dest_t, xn2)


def _ffn_kernel(bexp_ref, nvalid_ref, xs_ref, w1_ref, w3_ref, w2_ref, o_ref, w1b, w3b, w2b):
    i = pl.program_id(0)
    nvalid = nvalid_ref[0]

    @pl.when(i < nvalid)
    def _():
        changed = jnp.logical_or(i == 0, bexp_ref[i] != bexp_ref[jnp.maximum(i - 1, 0)])

        @pl.when(changed)
        def _():
            w1b[...] = w1_ref[0].astype(BF16)
            w3b[...] = w3_ref[0].astype(BF16)
            w2b[...] = w2_ref[0].astype(BF16)

        x_lo, x_hi = _unpack_halves(xs_ref[...])
        x_lo = x_lo.astype(BF16)
        x_hi = x_hi.astype(BF16)
        half = x_lo.shape[1]

        def up(wb):
            return (jnp.dot(x_lo, wb[0:half, :], preferred_element_type=F32)
                    + jnp.dot(x_hi, wb[half:2 * half, :], preferred_element_type=F32))

        a = up(w1b)
        g = up(w3b)
        hb = (a * _sigmoid(a) * g).astype(BF16)
        o_ref[...] = _pack_halves(jnp.dot(hb, w2b[...], preferred_element_type=F32))

    @pl.when(i >= nvalid)
    def _():
        o_ref[...] = jnp.zeros_like(o_ref)


def _ffn(block_expert, nvalid, xs, w1, w3, w2):
    p_rows, dp = xs.shape
    tb = ROW_BLOCK
    _, d, f = w1.shape
    grid_spec = pltpu.PrefetchScalarGridSpec(
        num_scalar_prefetch=2,
        grid=(p_rows // tb,),
        in_specs=[
            pl.BlockSpec((tb, dp), lambda i, be, nv: (jnp.minimum(i, nv[0] - 1), 0)),
            pl.BlockSpec((1, d, f), lambda i, be, nv: (be[i], 0, 0)),
            pl.BlockSpec((1, d, f), lambda i, be, nv: (be[i], 0, 0)),
            pl.BlockSpec((1, f, d), lambda i, be, nv: (be[i], 0, 0)),
        ],
        out_specs=pl.BlockSpec((tb, dp), lambda i, be, nv: (i, 0)),
        scratch_shapes=[
            pltpu.VMEM((d, f), BF16),
            pltpu.VMEM((d, f), BF16),
            pltpu.VMEM((f, d), BF16),
        ],
    )
    return pl.pallas_call(
        _ffn_kernel,
        grid_spec=grid_spec,
        out_shape=jax.ShapeDtypeStruct((p_rows, dp), U32),
        compiler_params=pltpu.CompilerParams(
            dimension_semantics=("arbitrary",), vmem_limit_bytes=VMEM_LIMIT),
        name="ffn",
    )(block_expert, nvalid, xs, w1, w3, w2)


def _combine_kernel(dest_hbm, y_hbm, x2_ref, gate_ref, g_ref, out_ref, dsm, ybuf, dsem, gsem):
    j = pl.program_id(0)
    nt = pl.num_programs(0) - 1
    tm = x2_ref.shape[0]

    def dest_copy(tile, k):
        buf = tile % 3
        return pltpu.make_async_copy(dest_hbm.at[k, pl.ds(tile * tm, tm)], dsm.at[buf, k],
                                     dsem.at[buf, k])

    def row_copy(r, k):
        return pltpu.make_async_copy(y_hbm.at[pl.ds(dsm[j % 3, k, r], 1), :],
                                     ybuf.at[j % 2, k, pl.ds(r, 1), :], gsem.at[j % 2])

    @pl.when(j == 0)
    def _():
        for k in range(MOE_TOPK):
            dest_copy(0, k).start()

    @pl.when(j < nt)
    def _():
        for k in range(MOE_TOPK):
            dest_copy(j, k).wait()

            @pl.when(j + 1 < nt)
            def _():
                dest_copy(j + 1, k).start()

        for r in range(tm):
            for k in range(MOE_TOPK):
                row_copy(r, k).start()

    @pl.when(j >= 1)
    def _():
        buf = (j - 1) % 2
        for k in range(MOE_TOPK):
            pltpu.make_async_copy(y_hbm.at[pl.ds(0, tm), :], ybuf.at[buf, k], gsem.at[buf]).wait()
        gate = gate_ref[...]
        lo0, hi0 = _unpack_halves(ybuf[buf, 0])
        lo1, hi1 = _unpack_halves(ybuf[buf, 1])
        y = jnp.concatenate([gate[:, 0:1] * lo0 + gate[:, 1:2] * lo1,
                             gate[:, 0:1] * hi0 + gate[:, 1:2] * hi1], axis=1)
        out_ref[...] = _rms(x2_ref[...] + y, g_ref[...])


def _combine(dest_t, y, x2, gates, g):
    n, d = x2.shape
    tm = MOE_TILE
    nt = n // tm
    prev = lambda j: (jnp.maximum(j - 1, 0), 0)
    return pl.pallas_call(
        _combine_kernel,
        grid=(nt + 1,),
        in_specs=[
            pl.BlockSpec(memory_space=pl.ANY),
            pl.BlockSpec(memory_space=pl.ANY),
            pl.BlockSpec((tm, d), prev),
            pl.BlockSpec((tm, LANES), prev),
            pl.BlockSpec((1, d), lambda j: (0, 0)),
        ],
        out_specs=pl.BlockSpec((tm, d), prev),
        out_shape=jax.ShapeDtypeStruct((n, d), F32),
        scratch_shapes=[
            pltpu.SMEM((3, MOE_TOPK, tm), I32),
            pltpu.VMEM((2, MOE_TOPK, tm, y.shape[1]), y.dtype),
            pltpu.SemaphoreType.DMA((3, MOE_TOPK)),
            pltpu.SemaphoreType.DMA((2,)),
        ],
        compiler_params=pltpu.CompilerParams(
            dimension_semantics=("arbitrary",), vmem_limit_bytes=VMEM_LIMIT),
        name="combine",
    )(dest_t, y, x2, gates, g)


def kernel(x, mem, norm_mix_g, w_in, b_gate, b_if, conv_w, conv_b, mh_norm_g, sgu_norm_g, w_s, b_s, w_out, norm_x_g, norm_mem_g, w_xq, w_xkv, w_xo, norm_moe_g, w_rg, b_rg, w_re, b_re, w1, w3, w2, norm_f_g):
    assert w_in.shape[0] == 1, "the combine stage fuses the final norm: one layer only"
    (norm_mix_g, w_in, b_gate, b_if, conv_w, conv_b, mh_norm_g, sgu_norm_g, w_s, b_s, w_out, norm_x_g,
     norm_mem_g, w_xq, w_xkv, w_xo, norm_moe_g, w_rg, b_rg, w_re, b_re, w1, w3, w2) = (
        p[0] for p in (norm_mix_g, w_in, b_gate, b_if, conv_w, conv_b, mh_norm_g, sgu_norm_g, w_s, b_s,
                       w_out, norm_x_g, norm_mem_g, w_xq, w_xkv, w_xo, norm_moe_g, w_rg, b_rg, w_re,
                       b_re, w1, w3, w2))
    b, s, d = x.shape
    n = b * s
    H = MLSTM_HEADS
    x2d = x.reshape(n, d)
    row = lambda v: v.reshape(1, -1)

    c_if = 4 * d
    w_main = jnp.concatenate([w_in[:, :c_if], w_in[:, c_if + 2 * H:]], axis=1).astype(BF16)
    w_if = w_in[:, c_if:c_if + 2 * H]
    w_if_col = jnp.pad(w_if, ((0, 0), (0, LANES - 2 * H))).astype(BF16)
    b_col = jnp.pad(b_if, (0, LANES - 2 * H)).reshape(1, LANES)

    proj, gc = _inproj(x2d, row(norm_mix_g), w_main, w_if_col, b_col, tm=min(1024, n), tn=2048)
    ya = _mlstm(proj.reshape(b, s, -1), gc.reshape(b, s, LANES), conv_w, row(conv_b),
                row(mh_norm_g), d)
    x1 = _merge(proj, ya.reshape(n, d), x2d, w_s, b_s.T, row(sgu_norm_g), row(b_gate),
                w_out.astype(BF16), tm=min(512, n))

    kv = _memkv(mem, row(norm_mem_g), w_xkv.astype(BF16))
    n_route = MOE_GROUPS + N_EXPERTS
    w_r = jnp.pad(jnp.concatenate([w_rg, w_re], axis=1), ((0, 0), (0, LANES - n_route)))
    b_r = jnp.pad(jnp.concatenate([b_rg, b_re]), (0, LANES - n_route)).reshape(1, LANES)
    x2, xn2, logits = _xattn(x1.reshape(b, s, d), kv, row(norm_x_g), w_xq.astype(BF16),
                             w_xo.astype(BF16), row(norm_moe_g), w_r, b_r, tm=min(512, s))
    x2 = x2.reshape(n, d)
    xn2 = xn2.reshape(n, d // 2)

    ids, gates, counts = _router(logits.reshape(n, LANES), tm=min(512, n))
    dest_t = _plan(ids, counts, tm=min(512, n))

    tb = ROW_BLOCK
    nb = n * MOE_TOPK // tb + N_EXPERTS
    cnt = counts[0, MOE_GROUPS:n_route].astype(I32)
    padded = (cnt + tb - 1) // tb * tb
    pends = jnp.cumsum(padded)
    block_start = jnp.arange(nb, dtype=I32) * tb
    block_expert = jnp.minimum(jnp.sum(pends[None, :] <= block_start[:, None], axis=1),
                               N_EXPERTS - 1).astype(I32)
    nvalid = (pends[-1] // tb).astype(I32).reshape(1)
    pad_start = (pends - padded + cnt).astype(I32)
    pad_len = (padded - cnt).astype(I32)

    xs = _dispatch(pad_start, pad_len, nvalid, dest_t, xn2, nb * tb)
    y = _ffn(block_expert, nvalid, xs, w1, w3, w2)
    return _combine(dest_t, y, x2, gates, row(norm_f_g)).reshape(b, s, d)
```
